```python
import jax, jax.numpy as jnp
from jax import lax
import numpy as np

D_MODEL = 1024
BATCH = 16
SEQ = 4096
DEPTH = 4

CTX_LEN = 256
GRID_W = 64
D_FF = 2816
N_MOD = 9
EPS = 1e-6
D_MIX = 1024
F_GROUPS = 4
F_GROUP_DIM = 64
F_W = F_GROUPS * F_GROUP_DIM
CONV_HEADS = 4
CONV_HEAD_DIM = 64
CONV_W = CONV_HEADS * CONV_HEAD_DIM
CONV_K = 3
MLA_HEADS = 8
QK_NOPE = 64
QK_ROPE = 32
V_DIM = 64
Q_LORA = 384
KV_LORA = 256
MLA_OUT = MLA_HEADS * V_DIM
ROPE_BASE = 10000.0
AXIS_ROPE = QK_ROPE // 2
Q_BLOCK = 128
ATTN_SCALE = (QK_NOPE + QK_ROPE) ** -0.5
OFF_F = 0
OFF_CB = OFF_F + F_W
OFF_CC = OFF_CB + CONV_W
OFF_CX = OFF_CC + CONV_W
OFF_Q = OFF_CX + CONV_W
OFF_KV = OFF_Q + Q_LORA
OFF_KR = OFF_KV + KV_LORA
IN_COLS = OFF_KR + QK_ROPE

kernel_name = "hybrid_fourier_conv_mla_macaron_dit"


def rmsnorm(x, g):
    xf = x.astype(jnp.float32)
    y = xf * lax.rsqrt(jnp.mean(xf * xf, axis=-1, keepdims=True) + EPS)
    return (y * g.astype(jnp.float32)).astype(x.dtype)


def modulation(cond, w_mod, b_mod):
    m = jax.nn.silu(cond) @ w_mod + b_mod
    return m.reshape(cond.shape[:-1] + (N_MOD, D_MODEL))


def modulate(h, m, i):
    return h * (1.0 + m[:, 3 * i + 1][:, None]) + m[:, 3 * i][:, None]


def gate(m, i):
    return m[:, 3 * i + 2][:, None]


def ffn_half_step(s, m, i, norm_g, w_in, w_out):
    h = modulate(rmsnorm(s, norm_g), m, i)
    gu = h @ w_in
    y = (jax.nn.silu(gu[..., :D_FF]) * gu[..., D_FF:]) @ w_out
    return s + 0.5 * gate(m, i) * y


def axial_rope_tables(rows, dtype):
    row = jnp.repeat(jnp.arange(rows), GRID_W).astype(jnp.float32)
    col = jnp.tile(jnp.arange(GRID_W), rows).astype(jnp.float32)
    inv = 1.0 / (ROPE_BASE ** (jnp.arange(0, AXIS_ROPE, 2, dtype=jnp.float32) / AXIS_ROPE))
    ang = jnp.stack([row[:, None] * inv, col[:, None] * inv], axis=1)
    return jnp.cos(ang).astype(dtype), jnp.sin(ang).astype(dtype)


def apply_axial_rope(x, cos, sin):
    shp = x.shape
    xr = x.reshape(shp[:-1] + (2, 2, AXIS_ROPE // 2))
    x1, x2 = xr[..., 0, :], xr[..., 1, :]
    if x.ndim == 4:
        cos, sin = cos[:, None], sin[:, None]
    out = jnp.stack([x1 * cos - x2 * sin, x2 * cos + x1 * sin], axis=-2)
    return out.reshape(shp)


def fourier_mix(u):
    b, n, _ = u.shape
    z = u.astype(jnp.float32).reshape(b, n, F_GROUPS, F_GROUP_DIM)
    z = jnp.fft.fft2(z, axes=(1, 3), norm="ortho").real
    return z.reshape(b, n, F_W).astype(u.dtype)


def conv_mix(gb, gc, v, conv_w):
    u = gc * v
    y = lax.conv_general_dilated(
        u, conv_w.astype(u.dtype)[:, None, :], window_strides=(1,),
        padding=((CONV_K // 2, CONV_K // 2),),
        dimension_numbers=("NWC", "WIO", "NWC"), feature_group_count=CONV_W)
    return gb * y


def mla_kv(p_kv, kv_norm, w_ukv, cos, sin):
    b, n = p_kv.shape[:2]
    kv_lat, k_rope = p_kv[..., :KV_LORA], p_kv[..., KV_LORA:]
    kv = (rmsnorm(kv_lat, kv_norm) @ w_ukv).reshape(b, n, MLA_HEADS, QK_NOPE + V_DIM)
    k_nope, v = kv[..., :QK_NOPE], kv[..., QK_NOPE:]
    if cos is not None:
        k_rope = apply_axial_rope(k_rope, cos, sin)
    return k_nope, k_rope, v


def mla_q(q_lat, q_norm, w_uq, cos, sin):
    b, n = q_lat.shape[:2]
    q = (rmsnorm(q_lat, q_norm) @ w_uq).reshape(b, n, MLA_HEADS, QK_NOPE + QK_ROPE)
    q_nope, q_rope = q[..., :QK_NOPE], q[..., QK_NOPE:]
    if cos is not None:
        q_rope = apply_axial_rope(q_rope, cos, sin)
    return q_nope, q_rope


def attend(q_nope, q_rope, k_nope, k_rope, v):
    s = (jnp.einsum("bqhd,bkhd->bhqk", q_nope, k_nope)
         + jnp.einsum("bqhr,bkr->bhqk", q_rope, k_rope)).astype(jnp.float32) * ATTN_SCALE
    p = jax.nn.softmax(s, axis=-1).astype(v.dtype)
    return jnp.einsum("bhqk,bkhd->bqhd", p, v)


def attend_blocked(q_nope, q_rope, k_nope, k_rope, v):
    b, n, h, _ = q_nope.shape
    nb = n // Q_BLOCK

    def to_blocks(t):
        return t.reshape((b, nb, Q_BLOCK) + t.shape[2:]).swapaxes(0, 1)

    out = lax.map(lambda qs: attend(qs[0], qs[1], k_nope, k_rope, v),
                  (to_blocks(q_nope), to_blocks(q_rope)))
    return out.swapaxes(0, 1).reshape(b, n, h * V_DIM)


def head_groups(proj, attn_out, conv_w):
    f = fourier_mix(proj[..., OFF_F:OFF_CB])
    cv = conv_mix(proj[..., OFF_CB:OFF_CC], proj[..., OFF_CC:OFF_CX], proj[..., OFF_CX:OFF_Q], conv_w)
    return jnp.concatenate([f, cv, attn_out], axis=-1)


def setup_inputs(seed: int = 0) -> dict:
    key = jax.random.key(seed)
    ks = jax.random.split(key, 24)
    L, D = DEPTH, D_MODEL

    def nrm(k, shape, scale):
        return jax.random.normal(k, shape, jnp.float32) * scale

    def gain(k, shape):
        return 1.0 + 0.02 * jax.random.normal(k, shape, jnp.float32)

    return {
        "x": nrm(ks[0], (BATCH, SEQ, D), 1.0),
        "c": nrm(ks[1], (BATCH, D), 1.0),
        "ctx": nrm(ks[2], (BATCH, CTX_LEN, D), 1.0),
        "c_ctx": nrm(ks[3], (D,), 1.0),
        "w_mod": nrm(ks[4], (L, D, N_MOD * D), 0.5 * D ** -0.5),
        "b_mod": nrm(ks[5], (L, N_MOD * D), 0.02),
        "ffn1_norm": gain(ks[6], (L, D)),
        "ffn1_w_in": nrm(ks[7], (L, D, 2 * D_FF), D ** -0.5),
        "ffn1_w_out": nrm(ks[8], (L, D_FF, D), D_FF ** -0.5),
        "mix_norm": gain(ks[9], (L, D)),
        "w_in": nrm(ks[10], (L, D, IN_COLS), D ** -0.5),
        "conv_w": nrm(ks[11], (L, CONV_K, CONV_W), CONV_K ** -0.5),
        "q_norm": gain(ks[12], (L, Q_LORA)),
        "w_uq": nrm(ks[13], (L, Q_LORA, MLA_HEADS * (QK_NOPE + QK_ROPE)), Q_LORA ** -0.5),
        "kv_norm": gain(ks[14], (L, KV_LORA)),
        "w_ukv": nrm(ks[15], (L, KV_LORA, MLA_HEADS * (QK_NOPE + V_DIM)), KV_LORA ** -0.5),
        "w_out": nrm(ks[16], (L, D_MIX, D), D_MIX ** -0.5),
        "ffn2_norm": gain(ks[17], (L, D)),
        "ffn2_w_in": nrm(ks[18], (L, D, 2 * D_FF), D ** -0.5),
        "ffn2_w_out": nrm(ks[19], (L, D_FF, D), D_FF ** -0.5),
        "final_norm": gain(ks[20], (D,)),
    }


def reference(x, c, ctx, c_ctx, w_mod, b_mod, ffn1_norm, ffn1_w_in, ffn1_w_out,
              mix_norm, w_in, conv_w, q_norm, w_uq, kv_norm, w_ukv, w_out,
              ffn2_norm, ffn2_w_in, ffn2_w_out, final_norm):
    n_lat = x.shape[1]
    rows = n_lat // GRID_W
    cos, sin = axial_rope_tables(rows, x.dtype)
    xs, cs = x, ctx
    for l in range(DEPTH):
        last = l == DEPTH - 1
        m_x = modulation(c, w_mod[l], b_mod[l])
        m_c = modulation(c_ctx, w_mod[l], b_mod[l])[None]

        xs = ffn_half_step(xs, m_x, 0, ffn1_norm[l], ffn1_w_in[l], ffn1_w_out[l])
        cs = ffn_half_step(cs, m_c, 0, ffn1_norm[l], ffn1_w_in[l], ffn1_w_out[l])

        hx = modulate(rmsnorm(xs, mix_norm[l]), m_x, 1)
        hc = modulate(rmsnorm(cs, mix_norm[l]), m_c, 1)
        px = hx @ w_in[l]
        if last:
            pc_kv = hc @ w_in[l][:, OFF_KV:]
        else:
            pc = hc @ w_in[l]
            pc_kv = pc[..., OFF_KV:]
        kn_c, kr_c, v_c = mla_kv(pc_kv, kv_norm[l], w_ukv[l], None, None)
        kn_x, kr_x, v_x = mla_kv(px[..., OFF_KV:], kv_norm[l], w_ukv[l], cos, sin)
        qn_x, qr_x = mla_q(px[..., OFF_Q:OFF_KV], q_norm[l], w_uq[l], cos, sin)
        att_x = attend_blocked(qn_x, qr_x,
                               jnp.concatenate([kn_x, kn_c], axis=1),
                               jnp.concatenate([kr_x, kr_c], axis=1),
                               jnp.concatenate([v_x, v_c], axis=1))
        mix_x = head_groups(px, att_x, conv_w[l]) @ w_out[l]
        xs = xs + gate(m_x, 1) * mix_x
        xs = ffn_half_step(xs, m_x, 2, ffn2_norm[l], ffn2_w_in[l], ffn2_w_out[l])

        if not last:
            qn_c, qr_c = mla_q(pc[..., OFF_Q:OFF_KV], q_norm[l], w_uq[l], None, None)
            att_c = attend(qn_c, qr_c, kn_c, kr_c, v_c).reshape(cs.shape[0], cs.shape[1], MLA_OUT)
            mix_c = head_groups(pc, att_c, conv_w[l]) @ w_out[l]
            cs = cs + gate(m_c, 1) * mix_c
            cs = ffn_half_step(cs, m_c, 2, ffn2_norm[l], ffn2_w_in[l], ffn2_w_out[l])
    return rmsnorm(xs, final_norm)
```

```python
import functools

import numpy as np
import jax
import jax.numpy as jnp
from jax import lax
from jax.experimental import pallas as pl
from jax.experimental.pallas import tpu as pltpu

D_MODEL = 1024
D_FF = 2816
N_MOD = 9
EPS = 1e-6
GRID_W = 64
F_W = 256
F_GROUPS = 4
F_GROUP_DIM = 64
CONV_W = 256
MLA_HEADS = 8
QK_NOPE = 64
QK_ROPE = 32
V_DIM = 64
Q_LORA = 384
KV_LORA = 256
MLA_OUT = MLA_HEADS * V_DIM
ROPE_BASE = 10000.0
AXIS_ROPE = QK_ROPE // 2
ATTN_SCALE = (QK_NOPE + QK_ROPE) ** -0.5
OFF_Q = 1024
OFF_KV = OFF_Q + Q_LORA
OFF_KR = OFF_KV + KV_LORA

LANES = 128
SUBLANES = 8
HEAD_PAD = LANES
QK_W = MLA_HEADS * HEAD_PAD
PAIR_W = 2 * HEAD_PAD
FF_CHUNK = 256
VMEM_LIMIT = 56 * 1024 * 1024

BF16 = jnp.bfloat16
F32 = jnp.float32


def _dot(a, b):
    return jnp.dot(a, b, preferred_element_type=F32)


def _dot_nt(a, b):
    return lax.dot_general(a, b, (((1,), (1,)), ((), ())), preferred_element_type=F32)


def _rms(x, g):
    return x * lax.rsqrt(jnp.mean(x * x, axis=-1, keepdims=True) + EPS) * g


def _params(n_axes):
    return pltpu.CompilerParams(dimension_semantics=("arbitrary",) * n_axes,
                                vmem_limit_bytes=VMEM_LIMIT)


def _const_spec(shape):
    nd = len(shape)
    return pl.BlockSpec(shape, lambda *_: (0,) * nd, pipeline_mode=pl.Buffered(1))


def _mod_kernel(c_ref, w_ref, b_ref, o_ref):
    c = c_ref[...]
    a = (c * jax.nn.sigmoid(c)).astype(BF16)
    o_ref[...] = _dot(a, w_ref[...].astype(BF16)) + b_ref[...]


def _modulation(cond, w_mod, b_mod):
    n_layers, d, n = w_mod.shape
    r = cond.shape[0]
    tn = 1024
    return pl.pallas_call(
        _mod_kernel,
        out_shape=jax.ShapeDtypeStruct((n_layers, r, n), F32),
        grid=(n_layers, n // tn),
        in_specs=[
            pl.BlockSpec((r, d), lambda l, j: (0, 0)),
            pl.BlockSpec((None, d, tn), lambda l, j: (l, 0, j)),
            pl.BlockSpec((None, 1, tn), lambda l, j: (l, 0, j)),
        ],
        out_specs=pl.BlockSpec((None, r, tn), lambda l, j: (l, 0, j)),
        compiler_params=_params(2),
        name="modulation",
    )(cond, w_mod, b_mod.reshape(n_layers, 1, n))


def _swiglu_ffn(h, w_in_ref, w_out_ref):
    acc = jnp.zeros((h.shape[0], D_MODEL), F32)
    for c in range(D_FF // FF_CHUNK):
        lo = c * FF_CHUNK
        g = _dot(h, w_in_ref[:, lo:lo + FF_CHUNK])
        u = _dot(h, w_in_ref[:, D_FF + lo:D_FF + lo + FF_CHUNK])
        a = (g * jax.nn.sigmoid(g) * u).astype(BF16)
        acc = acc + _dot(a, w_out_ref[lo:lo + FF_CHUNK, :])
    return acc


def _ffn_kernel(*refs, tm, seq, ctx, n_x_tiles, mod_idx, pre_mix, final_norm):
    it = iter(refs)
    s_ref, m_ref, g_ref, w_in_ref, w_out_ref = (next(it) for _ in range(5))
    if pre_mix:
        fz_ref, gb_ref, u_ref, up_ref, un_ref, att_ref, cw_ref, wo_ref = (next(it) for _ in range(8))
    if final_norm:
        fg_ref = next(it)
    o_ref = next(it)

    s = s_ref[...]
    if pre_mix:
        i = pl.program_id(0)
        u = u_ref[...]
        row = lax.broadcasted_iota(jnp.int32, u.shape, 0)
        u_prev = jnp.where(row == 0, up_ref[SUBLANES - 1:SUBLANES, :], pltpu.roll(u, 1, axis=0))
        u_next = jnp.where(row == tm - 1, un_ref[0:1, :], pltpu.roll(u, tm - 1, axis=0))
        grow = row + i * tm
        len_mask = jnp.where(i >= n_x_tiles, ctx - 1, seq - 1)
        pos = grow & len_mask
        u_prev = jnp.where(pos == 0, 0.0, u_prev)
        u_next = jnp.where(pos == len_mask, 0.0, u_next)
        cv = gb_ref[...] * (cw_ref[0:1, :] * u_prev + cw_ref[1:2, :] * u + cw_ref[2:3, :] * u_next)
        cat = jnp.concatenate([fz_ref[...], cv.astype(BF16), att_ref[...]], axis=-1)
        s = s + m_ref[5:6, :] * _dot(cat, wo_ref[...])

    k = 3 * mod_idx
    h = _rms(s, g_ref[...]) * (1.0 + m_ref[k + 1:k + 2, :]) + m_ref[k:k + 1, :]
    y = _swiglu_ffn(h.astype(BF16), w_in_ref, w_out_ref)
    out = s + 0.5 * m_ref[k + 2:k + 3, :] * y
    if final_norm:
        out = _rms(out, fg_ref[...])
    o_ref[...] = out


def _ffn_call(s, m, g, w_in, w_out, *, geo, mod_idx, mix=None, final_g=None, x_only=False):
    tm, seq, ctx, batch = geo["tm"], geo["seq"], geo["ctx"], geo["batch"]
    t_all = s.shape[0]
    tps = seq // tm
    n_x_tiles = batch * tps
    n_tiles = n_x_tiles if x_only else t_all // tm
    rows8 = t_all // SUBLANES

    row_spec = lambda w: pl.BlockSpec((tm, w), lambda i: (i, 0))
    in_specs = [
        row_spec(D_MODEL),
        pl.BlockSpec((None, N_MOD, D_MODEL), lambda i: (jnp.minimum(i // tps, batch), 0, 0)),
        _const_spec((1, D_MODEL)),
        _const_spec((D_MODEL, 2 * D_FF)),
        _const_spec((D_FF, D_MODEL)),
    ]
    args = [s, m, g.reshape(1, D_MODEL), w_in, w_out]
    if mix is not None:
        fz, gb, u, att, conv_w, w_o = mix
        blk = tm // SUBLANES
        in_specs += [
            row_spec(F_W), row_spec(CONV_W), row_spec(CONV_W),
            pl.BlockSpec((SUBLANES, CONV_W), lambda i: (jnp.maximum(i * blk - 1, 0), 0)),
            pl.BlockSpec((SUBLANES, CONV_W), lambda i: (jnp.minimum((i + 1) * blk, rows8 - 1), 0)),
            row_spec(MLA_OUT),
            _const_spec((3, CONV_W)),
            _const_spec((D_MODEL, D_MODEL)),
        ]
        args += [fz, gb, u, u, u, att, conv_w, w_o]
    if final_g is not None:
        in_specs.append(_const_spec((1, D_MODEL)))
        args.append(final_g.reshape(1, D_MODEL))

    kern = functools.partial(_ffn_kernel, tm=tm, seq=seq, ctx=ctx, n_x_tiles=n_x_tiles, mod_idx=mod_idx,
                             pre_mix=mix is not None, final_norm=final_g is not None)
    return pl.pallas_call(
        kern,
        out_shape=jax.ShapeDtypeStruct((n_tiles * tm, D_MODEL), F32),
        grid=(n_tiles,),
        in_specs=in_specs,
        out_specs=row_spec(D_MODEL),
        compiler_params=_params(1),
        name="ffn_mix" if mix is not None else "ffn",
    )(*args)


def _proj_kernel(s_ref, m_ref, g_ref, w1_ref, qg_ref, wq_ref, kg_ref, wkv_ref, cos_ref, sin_ref, bdc_ref, bds_ref,
                 zc_ref, zs_ref, gb_ref, u_ref, q_ref, k_ref, v_ref):
    s = s_ref[...]
    h = _rms(s, g_ref[...]) * (1.0 + m_ref[4:5, :]) + m_ref[3:4, :]
    p = _dot(h.astype(BF16), w1_ref[...])

    f = p[:, 0:F_W].astype(BF16)
    zc_ref[...] = _dot(f, bdc_ref[...]).astype(BF16)
    zs_ref[...] = _dot(f, bds_ref[...]).astype(BF16)
    gb_ref[...] = p[:, F_W:F_W + CONV_W]
    u_ref[...] = p[:, F_W + CONV_W:F_W + 2 * CONV_W] * p[:, F_W + 2 * CONV_W:F_W + 3 * CONV_W]

    cos = cos_ref[...]
    sin = sin_ref[...]
    qn = _rms(p[:, OFF_Q:OFF_KV], qg_ref[...]).astype(BF16)
    qq = _dot(qn, wq_ref[...])
    kn = _rms(p[:, OFF_KV:OFF_KR], kg_ref[...]).astype(BF16)
    kk = _dot(kn, wkv_ref[...])
    kr = p[:, OFF_KR:OFF_KR + HEAD_PAD] * cos + p[:, OFF_KR + HEAD_PAD:OFF_KR + 2 * HEAD_PAD] * sin
    for hd in range(MLA_HEADS):
        sl = slice(hd * HEAD_PAD, (hd + 1) * HEAD_PAD)
        sl_rot = slice(QK_W + hd * HEAD_PAD, QK_W + (hd + 1) * HEAD_PAD)
        q_ref[:, sl] = ((qq[:, sl] * cos + qq[:, sl_rot] * sin) * ATTN_SCALE).astype(BF16)
        k_ref[:, sl] = (kk[:, sl] + kr).astype(BF16)
    v_ref[...] = kk[:, QK_W:].astype(BF16)


def _proj_call(s, m, g, w1, qg, wq, kg, wkv, cos_t, sin_t, bdc, bds, *, geo):
    tm, seq, batch = geo["tm"], geo["seq"], geo["batch"]
    t_all = s.shape[0]
    tps = seq // tm
    n_x_tiles = batch * tps
    row_spec = lambda w: pl.BlockSpec((tm, w), lambda i: (i, 0))
    tab_spec = pl.BlockSpec((tm, HEAD_PAD), lambda i: (jnp.where(i < n_x_tiles, i % tps, tps), 0))
    out_widths = [(F_W, BF16), (F_W, BF16), (CONV_W, F32), (CONV_W, F32), (QK_W, BF16), (QK_W, BF16), (MLA_OUT, BF16)]
    return pl.pallas_call(
        _proj_kernel,
        out_shape=[jax.ShapeDtypeStruct((t_all, w), dt) for w, dt in out_widths],
        grid=(t_all // tm,),
        in_specs=[
            row_spec(D_MODEL),
            pl.BlockSpec((None, N_MOD, D_MODEL), lambda i: (jnp.minimum(i // tps, batch), 0, 0)),
            _const_spec((1, D_MODEL)),
            _const_spec(w1.shape),
            _const_spec((1, Q_LORA)),
            _const_spec(wq.shape),
            _const_spec((1, KV_LORA)),
            _const_spec(wkv.shape),
            tab_spec, tab_spec,
            _const_spec(bdc.shape),
            _const_spec(bds.shape),
        ],
        out_specs=[row_spec(w) for w, _ in out_widths],
        compiler_params=_params(1),
        name="in_proj",
    )(s, m, g.reshape(1, D_MODEL), w1, qg.reshape(1, Q_LORA), wq, kg.reshape(1, KV_LORA), wkv, cos_t, sin_t, bdc, bds)


def _attn_kernel(q_ref, kx_ref, kc_ref, vx_ref, vc_ref, o_ref, *, nq):
    qi = pl.program_id(2)

    def head(hd, with_x):
        sl = slice(hd * HEAD_PAD, (hd + 1) * HEAD_PAD)
        q = q_ref[:, sl]
        sc = _dot_nt(q, kc_ref[:, sl])
        mx = jnp.max(sc, axis=-1, keepdims=True)
        if with_x:
            sx = _dot_nt(q, kx_ref[:, sl])
            mx = jnp.maximum(mx, jnp.max(sx, axis=-1, keepdims=True))
        pc = jnp.exp(sc - mx)
        den = jnp.sum(pc, axis=-1, keepdims=True)
        o = _dot(pc.astype(BF16), vc_ref[...])
        if with_x:
            px = jnp.exp(sx - mx)
            den = den + jnp.sum(px, axis=-1, keepdims=True)
            o = o + _dot(px.astype(BF16), vx_ref[...])
        return o / den

    def pair(with_x):
        o0 = head(0, with_x)
        o1 = head(1, with_x)
        lane = lax.broadcasted_iota(jnp.int32, o0.shape, 1)
        o_ref[...] = jnp.where(lane < V_DIM, o0, o1).astype(BF16)

    @pl.when(qi < nq)
    def _():
        pair(True)

    @pl.when(qi >= nq)
    def _():
        pair(False)


def _attn_call(q, k, v, *, geo, with_ctx_queries):
    seq, ctx, batch = geo["seq"], geo["ctx"], geo["batch"]
    t_all = q.shape[0]
    tq = ctx
    nq = seq // tq
    ctx_blk0 = batch * seq // ctx
    n_pairs = MLA_HEADS // 2
    q_row = lambda b, qi: jnp.where(qi < nq, b * nq + qi, ctx_blk0 + b)
    return pl.pallas_call(
        functools.partial(_attn_kernel, nq=nq),
        out_shape=jax.ShapeDtypeStruct((t_all, MLA_OUT), BF16),
        grid=(batch, n_pairs, nq + (1 if with_ctx_queries else 0)),
        in_specs=[
            pl.BlockSpec((tq, PAIR_W), lambda b, p, qi: (q_row(b, qi), p)),
            pl.BlockSpec((seq, PAIR_W), lambda b, p, qi: (b, p)),
            pl.BlockSpec((ctx, PAIR_W), lambda b, p, qi: (ctx_blk0 + b, p)),
            pl.BlockSpec((seq, HEAD_PAD), lambda b, p, qi: (b, p)),
            pl.BlockSpec((ctx, HEAD_PAD), lambda b, p, qi: (ctx_blk0 + b, p)),
        ],
        out_specs=pl.BlockSpec((tq, HEAD_PAD), lambda b, p, qi: (q_row(b, qi), p)),
        compiler_params=_params(3),
        name="attention",
    )(q, k, k, v, v)


def _dft_kernel(c_ref, s_ref, zc_ref, zs_ref, o_ref):
    o_ref[...] = (_dot(c_ref[...], zc_ref[...]) - _dot(s_ref[...], zs_ref[...])).astype(BF16)


def _dft_ctx_kernel(c_ref, s_ref, zc_ref, zs_ref, prev_ref, o_ref):
    del prev_ref
    o_ref[...] = (_dot(c_ref[...], zc_ref[...]) - _dot(s_ref[...], zs_ref[...])).astype(BF16)


def _dft_call(cm, sm, zc, zs, *, geo):
    seq, batch = geo["seq"], geo["batch"]
    tp = min(512, seq)
    npt = seq // tp
    return pl.pallas_call(
        _dft_kernel,
        out_shape=jax.ShapeDtypeStruct((zc.shape[0], F_W), BF16),
        grid=(npt, batch),
        in_specs=[
            pl.BlockSpec((tp, seq), lambda p, b: (p, 0)),
            pl.BlockSpec((tp, seq), lambda p, b: (p, 0)),
            pl.BlockSpec((seq, F_W), lambda p, b: (b, 0)),
            pl.BlockSpec((seq, F_W), lambda p, b: (b, 0)),
        ],
        out_specs=pl.BlockSpec((tp, F_W), lambda p, b: (b * npt + p, 0)),
        compiler_params=_params(2),
        name="dft_latent",
    )(cm, sm, zc, zs)


def _dft_ctx_call(cm, sm, zc, zs, fz, *, geo):
    seq, ctx, batch = geo["seq"], geo["ctx"], geo["batch"]
    blk0 = batch * seq // ctx
    z_spec = pl.BlockSpec((ctx, F_W), lambda b: (blk0 + b, 0))
    return pl.pallas_call(
        _dft_ctx_kernel,
        out_shape=jax.ShapeDtypeStruct(fz.shape, BF16),
        grid=(batch,),
        in_specs=[_const_spec((ctx, ctx)), _const_spec((ctx, ctx)), z_spec, z_spec,
                  pl.BlockSpec(memory_space=pl.ANY)],
        out_specs=z_spec,
        input_output_aliases={4: 0},
        compiler_params=_params(1),
        name="dft_context",
    )(cm, sm, zc, zs, fz)


def _dft_mats(n):
    k = jnp.arange(n, dtype=jnp.int32)[:, None]
    na = max(n // GRID_W, 1)
    nb = n // na
    a = jnp.arange(na, dtype=jnp.int32)[None, :]
    b = jnp.arange(nb, dtype=jnp.int32)[None, :]
    w = 2.0 * np.pi / n
    ang1 = ((k * a * nb) % n).astype(F32) * w
    ang2 = ((k * b) % n).astype(F32) * w
    c1, s1, c2, s2 = jnp.cos(ang1), jnp.sin(ang1), jnp.cos(ang2), jnp.sin(ang2)
    scale = n ** -0.5
    cm = (c1[:, :, None] * c2[:, None, :] - s1[:, :, None] * s2[:, None, :]).reshape(n, n) * scale
    sm = (s1[:, :, None] * c2[:, None, :] + c1[:, :, None] * s2[:, None, :]).reshape(n, n) * scale
    return cm.astype(BF16), sm.astype(BF16)


def _channel_dft_mats():
    j = np.arange(F_GROUP_DIM)
    ang = 2.0 * np.pi * ((j[:, None] * j[None, :]) % F_GROUP_DIM) / F_GROUP_DIM
    eye = np.eye(F_GROUPS)
    scale = F_GROUP_DIM ** -0.5
    bdc = np.kron(eye, np.cos(ang) * scale)
    bds = np.kron(eye, np.sin(ang) * scale)
    return jnp.asarray(bdc, BF16), jnp.asarray(bds, BF16)


def _rope_tables(seq, pad_rows):
    rows = seq // GRID_W
    row = jnp.repeat(jnp.arange(rows), GRID_W).astype(F32)
    col = jnp.tile(jnp.arange(GRID_W), rows).astype(F32)
    inv = 1.0 / (ROPE_BASE ** (jnp.arange(0, AXIS_ROPE, 2, dtype=F32) / AXIS_ROPE))
    ar = row[:, None] * inv
    ac = col[:, None] * inv
    cos32 = jnp.concatenate([jnp.cos(ar), jnp.cos(ar), jnp.cos(ac), jnp.cos(ac)], axis=1)
    sin32 = jnp.concatenate([jnp.sin(ar), jnp.sin(ar), jnp.sin(ac), jnp.sin(ac)], axis=1)
    ones = jnp.ones((seq, QK_NOPE), F32)
    tail = HEAD_PAD - QK_NOPE - QK_ROPE
    cos_t = jnp.concatenate([ones, cos32, jnp.ones((seq, tail), F32)], axis=1)
    sin_t = jnp.concatenate([jnp.zeros((seq, QK_NOPE), F32), sin32, jnp.zeros((seq, tail), F32)], axis=1)
    cos_t = jnp.concatenate([cos_t, jnp.ones((pad_rows, HEAD_PAD), F32)], axis=0)
    sin_t = jnp.concatenate([sin_t, jnp.zeros((pad_rows, HEAD_PAD), F32)], axis=0)
    return cos_t, sin_t


def _rot_half(w):
    h = AXIS_ROPE // 2
    r1, r2, c1, c2 = w[..., 0:h], w[..., h:2 * h], w[..., 2 * h:3 * h], w[..., 3 * h:4 * h]
    return jnp.concatenate([-r2, r1, -c2, c1], axis=-1)


def _layer_weights(w_in, w_uq, w_ukv):
    d = w_in.shape[0]
    kr = w_in[:, OFF_KR:OFF_KR + QK_ROPE]
    pad_l = jnp.zeros((d, QK_NOPE), F32)
    pad_r = jnp.zeros((d, HEAD_PAD - QK_NOPE - QK_ROPE), F32)
    w1 = jnp.concatenate([w_in[:, :OFF_KR], pad_l, kr, pad_r, pad_l, _rot_half(kr), pad_r], axis=1).astype(BF16)

    uq = w_uq.reshape(Q_LORA, MLA_HEADS, QK_NOPE + QK_ROPE)
    zq = jnp.zeros((Q_LORA, MLA_HEADS, HEAD_PAD - QK_NOPE - QK_ROPE), F32)
    wq_a = jnp.concatenate([uq, zq], axis=-1).reshape(Q_LORA, QK_W)
    wq_b = jnp.concatenate([jnp.zeros((Q_LORA, MLA_HEADS, QK_NOPE), F32), _rot_half(uq[..., QK_NOPE:]), zq],
                           axis=-1).reshape(Q_LORA, QK_W)
    wq = jnp.concatenate([wq_a, wq_b], axis=1).astype(BF16)

    ukv = w_ukv.reshape(KV_LORA, MLA_HEADS, QK_NOPE + V_DIM)
    wk = jnp.concatenate([ukv[..., :QK_NOPE], jnp.zeros((KV_LORA, MLA_HEADS, HEAD_PAD - QK_NOPE), F32)],
                         axis=-1).reshape(KV_LORA, QK_W)
    wv = ukv[..., QK_NOPE:].reshape(KV_LORA, MLA_OUT)
    wkv = jnp.concatenate([wk, wv], axis=1).astype(BF16)
    return w1, wq, wkv


def kernel(x, c, ctx, c_ctx, w_mod, b_mod, ffn1_norm, ffn1_w_in, ffn1_w_out, mix_norm, w_in, conv_w, q_norm, w_uq,
           kv_norm, w_ukv, w_out, ffn2_norm, ffn2_w_in, ffn2_w_out, final_norm):
    batch, seq, d = x.shape
    n_ctx = ctx.shape[1]
    depth = w_mod.shape[0]
    assert d == D_MODEL and seq % GRID_W == 0 and seq % n_ctx == 0 and n_ctx % SUBLANES == 0
    assert seq & (seq - 1) == 0 and n_ctx & (n_ctx - 1) == 0
    tm = 512 if (seq % 512 == 0 and (batch * n_ctx) % 512 == 0) else n_ctx
    geo = dict(tm=tm, seq=seq, ctx=n_ctx, batch=batch)

    s = jnp.concatenate([x.reshape(batch * seq, d), ctx.reshape(batch * n_ctx, d)], axis=0)
    cond = jnp.concatenate([c, c_ctx[None, :]], axis=0)
    mods = _modulation(cond, w_mod, b_mod).reshape(depth, batch + 1, N_MOD, d)

    cos_t, sin_t = _rope_tables(seq, tm)
    cm_x, sm_x = _dft_mats(seq)
    cm_c, sm_c = _dft_mats(n_ctx)
    bdc, bds = _channel_dft_mats()

    for l in range(depth):
        last = l == depth - 1
        m = mods[l]
        w1, wq, wkv = _layer_weights(w_in[l], w_uq[l], w_ukv[l])
        s = _ffn_call(s, m, ffn1_norm[l], ffn1_w_in[l].astype(BF16), ffn1_w_out[l].astype(BF16), geo=geo, mod_idx=0)
        zc, zs, gb, u, q, k, v = _proj_call(s, m, mix_norm[l], w1, q_norm[l], wq, kv_norm[l], wkv, cos_t, sin_t,
                                            bdc, bds, geo=geo)
        att = _attn_call(q, k, v, geo=geo, with_ctx_queries=not last)
        fz = _dft_call(cm_x, sm_x, zc, zs, geo=geo)
        if not last:
            fz = _dft_ctx_call(cm_c, sm_c, zc, zs, fz, geo=geo)
        s = _ffn_call(s, m, ffn2_norm[l], ffn2_w_in[l].astype(BF16), ffn2_w_out[l].astype(BF16), geo=geo, mod_idx=2,
                      mix=(fz, gb, u, att, conv_w[l], w_out[l].astype(BF16)),
                      final_g=final_norm if last else None, x_only=last)
    return s.reshape(batch, seq, d)
```

```python
import functools

import numpy as np
import jax
import jax.numpy as jnp
from jax import lax
from jax.experimental import pallas as pl
from jax.experimental.pallas import tpu as pltpu

D_MODEL = 1024
D_FF = 2816
N_MOD = 9
EPS = 1e-6
GRID_W = 64
F_W = 256
F_GROUPS = 4
F_GROUP_DIM = 64
CONV_W = 256
MLA_HEADS = 8
QK_NOPE = 64
QK_ROPE = 32
V_DIM = 64
Q_LORA = 384
KV_LORA = 256
MLA_OUT = MLA_HEADS * V_DIM
ROPE_BASE = 10000.0
AXIS_ROPE = QK_ROPE // 2
ATTN_SCALE = (QK_NOPE + QK_ROPE) ** -0.5
OFF_Q = 1024
OFF_KV = OFF_Q + Q_LORA
OFF_KR = OFF_KV + KV_LORA

LANES = 128
SUBLANES = 8
HEAD_PAD = LANES
QK_W = MLA_HEADS * HEAD_PAD
PAIR_W = 2 * HEAD_PAD
VT_ROWS = V_DIM + 16
LOG2E = 1.4426950408889634
FF_CHUNK = 256
VMEM_LIMIT = 56 * 1024 * 1024

BF16 = jnp.bfloat16
F32 = jnp.float32


def _dot(a, b):
    return jnp.dot(a, b, preferred_element_type=F32)


def _dot_nt(a, b):
    return lax.dot_general(a, b, (((1,), (1,)), ((), ())), preferred_element_type=F32)


def _rms(x, g):
    return x * lax.rsqrt(jnp.mean(x * x, axis=-1, keepdims=True) + EPS) * g


def _params(n_axes):
    return pltpu.CompilerParams(dimension_semantics=("arbitrary",) * n_axes,
                                vmem_limit_bytes=VMEM_LIMIT)


def _const_spec(shape):
    nd = len(shape)
    return pl.BlockSpec(shape, lambda *_: (0,) * nd, pipeline_mode=pl.Buffered(1))


def _mod_kernel(c_ref, w_ref, b_ref, o_ref):
    c = c_ref[...]
    a = (c * jax.nn.sigmoid(c)).astype(BF16)
    o_ref[...] = _dot(a, w_ref[...].astype(BF16)) + b_ref[...]


def _modulation(cond, w_mod, b_mod):
    n_layers, d, n = w_mod.shape
    r = cond.shape[0]
    tn = 1024
    return pl.pallas_call(
        _mod_kernel,
        out_shape=jax.ShapeDtypeStruct((n_layers, r, n), F32),
        grid=(n_layers, n // tn),
        in_specs=[
            pl.BlockSpec((r, d), lambda l, j: (0, 0)),
            pl.BlockSpec((None, d, tn), lambda l, j: (l, 0, j)),
            pl.BlockSpec((None, 1, tn), lambda l, j: (l, 0, j)),
        ],
        out_specs=pl.BlockSpec((None, r, tn), lambda l, j: (l, 0, j)),
        compiler_params=_params(2),
        name="modulation",
    )(cond, w_mod, b_mod.reshape(n_layers, 1, n))


def _swiglu_ffn(h, w_in_ref, w_out_ref):
    acc = jnp.zeros((h.shape[0], D_MODEL), F32)
    for c in range(D_FF // FF_CHUNK):
        lo = c * FF_CHUNK
        g = _dot(h, w_in_ref[:, lo:lo + FF_CHUNK])
        u = _dot(h, w_in_ref[:, D_FF + lo:D_FF + lo + FF_CHUNK])
        a = (g * jax.nn.sigmoid(g) * u).astype(BF16)
        acc = acc + _dot(a, w_out_ref[lo:lo + FF_CHUNK, :])
    return acc


def _ffn_kernel(*refs, tm, seq, ctx, n_x_tiles, mod_idx, pre_mix, final_norm):
    it = iter(refs)
    s_ref, m_ref, g_ref, w_in_ref, w_out_ref = (next(it) for _ in range(5))
    if pre_mix:
        fz_ref, gb_ref, u_ref, up_ref, un_ref, att_ref, cw_ref, wo_ref = (next(it) for _ in range(8))
    if final_norm:
        fg_ref = next(it)
    o_ref = next(it)

    s = s_ref[...]
    if pre_mix:
        i = pl.program_id(0)
        u = u_ref[...]
        row = lax.broadcasted_iota(jnp.int32, u.shape, 0)
        u_prev = jnp.where(row == 0, up_ref[SUBLANES - 1:SUBLANES, :], pltpu.roll(u, 1, axis=0))
        u_next = jnp.where(row == tm - 1, un_ref[0:1, :], pltpu.roll(u, tm - 1, axis=0))
        grow = row + i * tm
        len_mask = jnp.where(i >= n_x_tiles, ctx - 1, seq - 1)
        pos = grow & len_mask
        u_prev = jnp.where(pos == 0, 0.0, u_prev)
        u_next = jnp.where(pos == len_mask, 0.0, u_next)
        cv = gb_ref[...] * (cw_ref[0:1, :] * u_prev + cw_ref[1:2, :] * u + cw_ref[2:3, :] * u_next)
        cat = jnp.concatenate([fz_ref[...], cv.astype(BF16), att_ref[...]], axis=-1)
        s = s + m_ref[5:6, :] * _dot(cat, wo_ref[...])

    k = 3 * mod_idx
    h = _rms(s, g_ref[...]) * (1.0 + m_ref[k + 1:k + 2, :]) + m_ref[k:k + 1, :]
    y = _swiglu_ffn(h.astype(BF16), w_in_ref, w_out_ref)
    out = s + 0.5 * m_ref[k + 2:k + 3, :] * y
    if final_norm:
        out = _rms(out, fg_ref[...])
    o_ref[...] = out


def _ffn_call(s, m, g, w_in, w_out, *, geo, mod_idx, mix=None, final_g=None, x_only=False):
    tm, seq, ctx, batch = geo["tm"], geo["seq"], geo["ctx"], geo["batch"]
    t_all = s.shape[0]
    tps = seq // tm
    n_x_tiles = batch * tps
    n_tiles = n_x_tiles if x_only else t_all // tm
    rows8 = t_all // SUBLANES

    row_spec = lambda w: pl.BlockSpec((tm, w), lambda i: (i, 0))
    in_specs = [
        row_spec(D_MODEL),
        pl.BlockSpec((None, N_MOD, D_MODEL), lambda i: (jnp.minimum(i // tps, batch), 0, 0)),
        _const_spec((1, D_MODEL)),
        _const_spec((D_MODEL, 2 * D_FF)),
        _const_spec((D_FF, D_MODEL)),
    ]
    args = [s, m, g.reshape(1, D_MODEL), w_in, w_out]
    if mix is not None:
        fz, gb, u, att, conv_w, w_o = mix
        blk = tm // SUBLANES
        in_specs += [
            row_spec(F_W), row_spec(CONV_W), row_spec(CONV_W),
            pl.BlockSpec((SUBLANES, CONV_W), lambda i: (jnp.maximum(i * blk - 1, 0), 0)),
            pl.BlockSpec((SUBLANES, CONV_W), lambda i: (jnp.minimum((i + 1) * blk, rows8 - 1), 0)),
            row_spec(MLA_OUT),
            _const_spec((3, CONV_W)),
            _const_spec((D_MODEL, D_MODEL)),
        ]
        args += [fz, gb, u, u, u, att, conv_w, w_o]
    if final_g is not None:
        in_specs.append(_const_spec((1, D_MODEL)))
        args.append(final_g.reshape(1, D_MODEL))

    kern = functools.partial(_ffn_kernel, tm=tm, seq=seq, ctx=ctx, n_x_tiles=n_x_tiles, mod_idx=mod_idx,
                             pre_mix=mix is not None, final_norm=final_g is not None)
    return pl.pallas_call(
        kern,
        out_shape=jax.ShapeDtypeStruct((n_tiles * tm, D_MODEL), F32),
        grid=(n_tiles,),
        in_specs=in_specs,
        out_specs=row_spec(D_MODEL),
        compiler_params=_params(1),
        name="ffn_mix" if mix is not None else "ffn",
    )(*args)


def _proj_kernel(s_ref, m_ref, g_ref, w1_ref, qg_ref, wq_ref, kg_ref, wk_ref, wvt_ref, one_ref, cos_ref, sin_ref,
                 bdc_ref, bds_ref, zc_ref, zs_ref, gb_ref, u_ref, q_ref, k_ref, vt_ref):
    s = s_ref[...]
    h = _rms(s, g_ref[...]) * (1.0 + m_ref[4:5, :]) + m_ref[3:4, :]
    p = _dot(h.astype(BF16), w1_ref[...])

    f = p[:, 0:F_W].astype(BF16)
    zc_ref[...] = _dot(f, bdc_ref[...]).astype(BF16)
    zs_ref[...] = _dot(f, bds_ref[...]).astype(BF16)
    gb_ref[...] = p[:, F_W:F_W + CONV_W]
    u_ref[...] = p[:, F_W + CONV_W:F_W + 2 * CONV_W] * p[:, F_W + 2 * CONV_W:F_W + 3 * CONV_W]

    cos = cos_ref[...]
    sin = sin_ref[...]
    qn = _rms(p[:, OFF_Q:OFF_KV], qg_ref[...]).astype(BF16)
    qq = _dot(qn, wq_ref[...])
    kn = _rms(p[:, OFF_KV:OFF_KR], kg_ref[...]).astype(BF16)
    kk = _dot(kn, wk_ref[...])
    kr = p[:, OFF_KR:OFF_KR + HEAD_PAD] * cos + p[:, OFF_KR + HEAD_PAD:OFF_KR + 2 * HEAD_PAD] * sin
    for hd in range(MLA_HEADS):
        sl = slice(hd * HEAD_PAD, (hd + 1) * HEAD_PAD)
        sl_rot = slice(QK_W + hd * HEAD_PAD, QK_W + (hd + 1) * HEAD_PAD)
        q_ref[:, sl] = ((qq[:, sl] * cos + qq[:, sl_rot] * sin) * (ATTN_SCALE * LOG2E)).astype(BF16)
        k_ref[:, sl] = (kk[:, sl] + kr).astype(BF16)
    vt_ref[...] = (_dot_nt(wvt_ref[...], kn) + one_ref[...]).astype(BF16)


def _proj_call(s, m, g, w1, qg, wq, kg, wk, wvt, one_col, cos_t, sin_t, bdc, bds, *, geo):
    tm, seq, batch = geo["tm"], geo["seq"], geo["batch"]
    t_all = s.shape[0]
    tps = seq // tm
    n_x_tiles = batch * tps
    row_spec = lambda w: pl.BlockSpec((tm, w), lambda i: (i, 0))
    tab_spec = pl.BlockSpec((tm, HEAD_PAD), lambda i: (jnp.where(i < n_x_tiles, i % tps, tps), 0))
    out_widths = [(F_W, BF16), (F_W, BF16), (CONV_W, F32), (CONV_W, F32), (QK_W, BF16), (QK_W, BF16)]
    vt_rows = MLA_HEADS * VT_ROWS
    return pl.pallas_call(
        _proj_kernel,
        out_shape=[jax.ShapeDtypeStruct((t_all, w), dt) for w, dt in out_widths]
        + [jax.ShapeDtypeStruct((vt_rows, t_all), BF16)],
        grid=(t_all // tm,),
        in_specs=[
            row_spec(D_MODEL),
            pl.BlockSpec((None, N_MOD, D_MODEL), lambda i: (jnp.minimum(i // tps, batch), 0, 0)),
            _const_spec((1, D_MODEL)),
            _const_spec(w1.shape),
            _const_spec((1, Q_LORA)),
            _const_spec(wq.shape),
            _const_spec((1, KV_LORA)),
            _const_spec(wk.shape),
            _const_spec(wvt.shape),
            _const_spec(one_col.shape),
            tab_spec, tab_spec,
            _const_spec(bdc.shape),
            _const_spec(bds.shape),
        ],
        out_specs=[row_spec(w) for w, _ in out_widths] + [pl.BlockSpec((vt_rows, tm), lambda i: (0, i))],
        compiler_params=_params(1),
        name="in_proj",
    )(s, m, g.reshape(1, D_MODEL), w1, qg.reshape(1, Q_LORA), wq, kg.reshape(1, KV_LORA), wk, wvt, one_col,
      cos_t, sin_t, bdc, bds)


def _attn_heads(qs, key_chunks):
    heads = range(len(qs))
    n_chunks = len(key_chunks[0])
    mx = [None for _ in heads]
    acc = [None for _ in heads]
    s_next = [_dot_nt(key_chunks[h][0][0], qs[h]) for h in heads]
    for c in range(n_chunks):
        for h in heads:
            s = s_next[h]
            if c + 1 < n_chunks:
                s_next[h] = _dot_nt(key_chunks[h][c + 1][0], qs[h])
            vt = key_chunks[h][c][1]
            cmax = jnp.max(s, axis=0, keepdims=True)
            if c == 0:
                mx[h] = cmax
                acc[h] = _dot(vt, jnp.exp2((s - cmax).astype(BF16)))
            else:
                new = jnp.maximum(mx[h], cmax)
                acc[h] = acc[h] * jnp.exp2(mx[h] - new) + _dot(vt, jnp.exp2((s - new).astype(BF16)))
                mx[h] = new
    return [a[0:V_DIM, :] / a[V_DIM:V_DIM + 1, :] for a in acc]


def _attn_kernel(q_ref, kx_ref, kc_ref, vtx_ref, vtc_ref, o_ref, *, key_chunk):
    qs, chunks = [], []
    for hd in range(2):
        sl = slice(hd * HEAD_PAD, (hd + 1) * HEAD_PAD)
        vsl = slice(hd * VT_ROWS, (hd + 1) * VT_ROWS)
        qs.append(q_ref[:, sl])
        chunks.append([(kc_ref[:, sl], vtc_ref[vsl, :])]
                      + [(kx_ref[lo:lo + key_chunk, sl], vtx_ref[vsl, lo:lo + key_chunk])
                         for lo in range(0, kx_ref.shape[0], key_chunk)])
    o_ref[...] = jnp.concatenate(_attn_heads(qs, chunks), axis=0).T.astype(BF16)


def _attn_ctx_kernel(q_ref, kc_ref, vtc_ref, prev_ref, o_ref):
    del prev_ref
    qs, chunks = [], []
    for hd in range(2):
        sl = slice(hd * HEAD_PAD, (hd + 1) * HEAD_PAD)
        qs.append(q_ref[:, sl])
        chunks.append([(kc_ref[:, sl], vtc_ref[hd * VT_ROWS:(hd + 1) * VT_ROWS, :])])
    o_ref[...] = jnp.concatenate(_attn_heads(qs, chunks), axis=0).T.astype(BF16)


def _attn_call(q, k, vt, *, geo):
    seq, ctx, batch = geo["seq"], geo["ctx"], geo["batch"]
    tq = min(512, seq)
    nq = seq // tq
    ctx_blk0 = batch * seq // ctx
    return pl.pallas_call(
        functools.partial(_attn_kernel, key_chunk=min(512, seq)),
        out_shape=jax.ShapeDtypeStruct((q.shape[0], MLA_OUT), BF16),
        grid=(batch, MLA_HEADS // 2, nq),
        in_specs=[
            pl.BlockSpec((tq, PAIR_W), lambda b, p, qi: (b * nq + qi, p)),
            pl.BlockSpec((seq, PAIR_W), lambda b, p, qi: (b, p)),
            pl.BlockSpec((ctx, PAIR_W), lambda b, p, qi: (ctx_blk0 + b, p)),
            pl.BlockSpec((2 * VT_ROWS, seq), lambda b, p, qi: (p, b)),
            pl.BlockSpec((2 * VT_ROWS, ctx), lambda b, p, qi: (p, ctx_blk0 + b)),
        ],
        out_specs=pl.BlockSpec((tq, HEAD_PAD), lambda b, p, qi: (b * nq + qi, p)),
        compiler_params=_params(3),
        name="attention",
    )(q, k, k, vt, vt)


def _attn_ctx_call(q, k, vt, att, *, geo):
    seq, ctx, batch = geo["seq"], geo["ctx"], geo["batch"]
    ctx_blk0 = batch * seq // ctx
    return pl.pallas_call(
        _attn_ctx_kernel,
        out_shape=jax.ShapeDtypeStruct(att.shape, BF16),
        grid=(batch, MLA_HEADS // 2),
        in_specs=[
            pl.BlockSpec((ctx, PAIR_W), lambda b, p: (ctx_blk0 + b, p)),
            pl.BlockSpec((ctx, PAIR_W), lambda b, p: (ctx_blk0 + b, p)),
            pl.BlockSpec((2 * VT_ROWS, ctx), lambda b, p: (p, ctx_blk0 + b)),
            pl.BlockSpec(memory_space=pl.ANY),
        ],
        out_specs=pl.BlockSpec((ctx, HEAD_PAD), lambda b, p: (ctx_blk0 + b, p)),
        input_output_aliases={3: 0},
        compiler_params=_params(2),
        name="attention_context",
    )(q, k, vt, att)


def _dft_kernel(c_ref, s_ref, zc_ref, zs_ref, o_ref):
    o_ref[...] = (_dot(c_ref[...], zc_ref[...]) - _dot(s_ref[...], zs_ref[...])).astype(BF16)


def _dft_ctx_kernel(c_ref, s_ref, zc_ref, zs_ref, prev_ref, o_ref):
    del prev_ref
    o_ref[...] = (_dot(c_ref[...], zc_ref[...]) - _dot(s_ref[...], zs_ref[...])).astype(BF16)


def _dft_call(cm, sm, zc, zs, *, geo):
    seq, batch = geo["seq"], geo["batch"]
    tp = min(512, seq)
    npt = seq // tp
    return pl.pallas_call(
        _dft_kernel,
        out_shape=jax.ShapeDtypeStruct((zc.shape[0], F_W), BF16),
        grid=(npt, batch),
        in_specs=[
            pl.BlockSpec((tp, seq), lambda p, b: (p, 0)),
            pl.BlockSpec((tp, seq), lambda p, b: (p, 0)),
            pl.BlockSpec((seq, F_W), lambda p, b: (b, 0)),
            pl.BlockSpec((seq, F_W), lambda p, b: (b, 0)),
        ],
        out_specs=pl.BlockSpec((tp, F_W), lambda p, b: (b * npt + p, 0)),
        compiler_params=_params(2),
        name="dft_latent",
    )(cm, sm, zc, zs)


def _dft_ctx_call(cm, sm, zc, zs, fz, *, geo):
    seq, ctx, batch = geo["seq"], geo["ctx"], geo["batch"]
    blk0 = batch * seq // ctx
    z_spec = pl.BlockSpec((ctx, F_W), lambda b: (blk0 + b, 0))
    return pl.pallas_call(
        _dft_ctx_kernel,
        out_shape=jax.ShapeDtypeStruct(fz.shape, BF16),
        grid=(batch,),
        in_specs=[_const_spec((ctx, ctx)), _const_spec((ctx, ctx)), z_spec, z_spec,
                  pl.BlockSpec(memory_space=pl.ANY)],
        out_specs=z_spec,
        input_output_aliases={4: 0},
        compiler_params=_params(1),
        name="dft_context",
    )(cm, sm, zc, zs, fz)


def _dft_mats(n):
    k = jnp.arange(n, dtype=jnp.int32)[:, None]
    na = max(n // GRID_W, 1)
    nb = n // na
    a = jnp.arange(na, dtype=jnp.int32)[None, :]
    b = jnp.arange(nb, dtype=jnp.int32)[None, :]
    w = 2.0 * np.pi / n
    ang1 = ((k * a * nb) % n).astype(F32) * w
    ang2 = ((k * b) % n).astype(F32) * w
    c1, s1, c2, s2 = jnp.cos(ang1), jnp.sin(ang1), jnp.cos(ang2), jnp.sin(ang2)
    scale = n ** -0.5
    cm = (c1[:, :, None] * c2[:, None, :] - s1[:, :, None] * s2[:, None, :]).reshape(n, n) * scale
    sm = (s1[:, :, None] * c2[:, None, :] + c1[:, :, None] * s2[:, None, :]).reshape(n, n) * scale
    return cm.astype(BF16), sm.astype(BF16)


def _channel_dft_mats():
    j = np.arange(F_GROUP_DIM)
    ang = 2.0 * np.pi * ((j[:, None] * j[None, :]) % F_GROUP_DIM) / F_GROUP_DIM
    eye = np.eye(F_GROUPS)
    scale = F_GROUP_DIM ** -0.5
    bdc = np.kron(eye, np.cos(ang) * scale)
    bds = np.kron(eye, np.sin(ang) * scale)
    return jnp.asarray(bdc, BF16), jnp.asarray(bds, BF16)


def _rope_tables(seq, pad_rows):
    rows = seq // GRID_W
    row = jnp.repeat(jnp.arange(rows), GRID_W).astype(F32)
    col = jnp.tile(jnp.arange(GRID_W), rows).astype(F32)
    inv = 1.0 / (ROPE_BASE ** (jnp.arange(0, AXIS_ROPE, 2, dtype=F32) / AXIS_ROPE))
    ar = row[:, None] * inv
    ac = col[:, None] * inv
    cos32 = jnp.concatenate([jnp.cos(ar), jnp.cos(ar), jnp.cos(ac), jnp.cos(ac)], axis=1)
    sin32 = jnp.concatenate([jnp.sin(ar), jnp.sin(ar), jnp.sin(ac), jnp.sin(ac)], axis=1)
    ones = jnp.ones((seq, QK_NOPE), F32)
    tail = HEAD_PAD - QK_NOPE - QK_ROPE
    cos_t = jnp.concatenate([ones, cos32, jnp.ones((seq, tail), F32)], axis=1)
    sin_t = jnp.concatenate([jnp.zeros((seq, QK_NOPE), F32), sin32, jnp.zeros((seq, tail), F32)], axis=1)
    cos_t = jnp.concatenate([cos_t, jnp.ones((pad_rows, HEAD_PAD), F32)], axis=0)
    sin_t = jnp.concatenate([sin_t, jnp.zeros((pad_rows, HEAD_PAD), F32)], axis=0)
    return cos_t, sin_t


def _rot_half(w):
    h = AXIS_ROPE // 2
    r1, r2, c1, c2 = w[..., 0:h], w[..., h:2 * h], w[..., 2 * h:3 * h], w[..., 3 * h:4 * h]
    return jnp.concatenate([-r2, r1, -c2, c1], axis=-1)


def _layer_weights(w_in, w_uq, w_ukv):
    d = w_in.shape[0]
    kr = w_in[:, OFF_KR:OFF_KR + QK_ROPE]
    pad_l = jnp.zeros((d, QK_NOPE), F32)
    pad_r = jnp.zeros((d, HEAD_PAD - QK_NOPE - QK_ROPE), F32)
    w1 = jnp.concatenate([w_in[:, :OFF_KR], pad_l, kr, pad_r, pad_l, _rot_half(kr), pad_r], axis=1).astype(BF16)

    uq = w_uq.reshape(Q_LORA, MLA_HEADS, QK_NOPE + QK_ROPE)
    zq = jnp.zeros((Q_LORA, MLA_HEADS, HEAD_PAD - QK_NOPE - QK_ROPE), F32)
    wq_a = jnp.concatenate([uq, zq], axis=-1).reshape(Q_LORA, QK_W)
    wq_b = jnp.concatenate([jnp.zeros((Q_LORA, MLA_HEADS, QK_NOPE), F32), _rot_half(uq[..., QK_NOPE:]), zq],
                           axis=-1).reshape(Q_LORA, QK_W)
    wq = jnp.concatenate([wq_a, wq_b], axis=1).astype(BF16)

    ukv = w_ukv.reshape(KV_LORA, MLA_HEADS, QK_NOPE + V_DIM)
    wk = jnp.concatenate([ukv[..., :QK_NOPE], jnp.zeros((KV_LORA, MLA_HEADS, HEAD_PAD - QK_NOPE), F32)],
                         axis=-1).reshape(KV_LORA, QK_W)
    wvt = jnp.concatenate([ukv[..., QK_NOPE:], jnp.zeros((KV_LORA, MLA_HEADS, VT_ROWS - V_DIM), F32)],
                          axis=-1).reshape(KV_LORA, MLA_HEADS * VT_ROWS).T
    return w1, wq, wk.astype(BF16), wvt.astype(BF16)


def kernel(x, c, ctx, c_ctx, w_mod, b_mod, ffn1_norm, ffn1_w_in, ffn1_w_out, mix_norm, w_in, conv_w, q_norm, w_uq,
           kv_norm, w_ukv, w_out, ffn2_norm, ffn2_w_in, ffn2_w_out, final_norm):
    batch, seq, d = x.shape
    n_ctx = ctx.shape[1]
    depth = w_mod.shape[0]
    assert d == D_MODEL and seq % GRID_W == 0 and seq % n_ctx == 0 and n_ctx % SUBLANES == 0
    assert seq & (seq - 1) == 0 and n_ctx & (n_ctx - 1) == 0
    tm = 512 if (seq % 512 == 0 and (batch * n_ctx) % 512 == 0) else n_ctx
    geo = dict(tm=tm, seq=seq, ctx=n_ctx, batch=batch)

    s = jnp.concatenate([x.reshape(batch * seq, d), ctx.reshape(batch * n_ctx, d)], axis=0)
    cond = jnp.concatenate([c, c_ctx[None, :]], axis=0)
    mods = _modulation(cond, w_mod, b_mod).reshape(depth, batch + 1, N_MOD, d)

    cos_t, sin_t = _rope_tables(seq, tm)
    cm_x, sm_x = _dft_mats(seq)
    cm_c, sm_c = _dft_mats(n_ctx)
    bdc, bds = _channel_dft_mats()
    one_col = jnp.asarray((np.arange(MLA_HEADS * VT_ROWS) % VT_ROWS >= V_DIM).astype(np.float32)[:, None])

    for l in range(depth):
        last = l == depth - 1
        m = mods[l]
        w1, wq, wk, wvt = _layer_weights(w_in[l], w_uq[l], w_ukv[l])
        s = _ffn_call(s, m, ffn1_norm[l], ffn1_w_in[l].astype(BF16), ffn1_w_out[l].astype(BF16), geo=geo, mod_idx=0)
        zc, zs, gb, u, q, k, vt = _proj_call(s, m, mix_norm[l], w1, q_norm[l], wq, kv_norm[l], wk, wvt, one_col,
                                             cos_t, sin_t, bdc, bds, geo=geo)
        att = _attn_call(q, k, vt, geo=geo)
        fz = _dft_call(cm_x, sm_x, zc, zs, geo=geo)
        if not last:
            att = _attn_ctx_call(q, k, vt, att, geo=geo)
            fz = _dft_ctx_call(cm_c, sm_c, zc, zs, fz, geo=geo)
        s = _ffn_call(s, m, ffn2_norm[l], ffn2_w_in[l].astype(BF16), ffn2_w_out[l].astype(BF16), geo=geo, mod_idx=2,
                      mix=(fz, gb, u, att, conv_w[l], w_out[l].astype(BF16)),
                      final_g=final_norm if last else None, x_only=last)
    return s.reshape(batch, seq, d)
```

```python
import functools

import numpy as np
import jax
import jax.numpy as jnp
from jax import lax
from jax.experimental import pallas as pl
from jax.experimental.pallas import tpu as pltpu

D_MODEL = 1024
D_FF = 2816
N_MOD = 9
EPS = 1e-6
GRID_W = 64
F_W = 256
F_GROUPS = 4
F_GROUP_DIM = 64
CONV_W = 256
MLA_HEADS = 8
QK_NOPE = 64
QK_ROPE = 32
V_DIM = 64
Q_LORA = 384
KV_LORA = 256
MLA_OUT = MLA_HEADS * V_DIM
ROPE_BASE = 10000.0
AXIS_ROPE = QK_ROPE // 2
ATTN_SCALE = (QK_NOPE + QK_ROPE) ** -0.5
OFF_Q = 1024
OFF_KV = OFF_Q + Q_LORA
OFF_KR = OFF_KV + KV_LORA

LANES = 128
SUBLANES = 8
HEAD_PAD = LANES
QK_W = MLA_HEADS * HEAD_PAD
PAIR_W = 2 * HEAD_PAD
VT_ROWS = V_DIM + 16
LOG2E = 1.4426950408889634
SCORE_LOOKAHEAD = 2
LAG_MAX_JUMP = 64.0
FF_CHUNK = 256
VMEM_LIMIT = 56 * 1024 * 1024

BF16 = jnp.bfloat16
F32 = jnp.float32


def _dot(a, b):
    return jnp.dot(a, b, preferred_element_type=F32)


def _dot_nt(a, b):
    return lax.dot_general(a, b, (((1,), (1,)), ((), ())), preferred_element_type=F32)


def _rms(x, g):
    return x * lax.rsqrt(jnp.mean(x * x, axis=-1, keepdims=True) + EPS) * g


def _params(n_axes, flags=None):
    return pltpu.CompilerParams(dimension_semantics=("arbitrary",) * n_axes,
                                vmem_limit_bytes=VMEM_LIMIT, flags=flags)


def _const_spec(shape):
    nd = len(shape)
    return pl.BlockSpec(shape, lambda *_: (0,) * nd, pipeline_mode=pl.Buffered(1))


def _mod_kernel(c_ref, w_ref, b_ref, o_ref):
    c = c_ref[...]
    a = (c * jax.nn.sigmoid(c)).astype(BF16)
    o_ref[...] = _dot(a, w_ref[...].astype(BF16)) + b_ref[...]


def _modulation(cond, w_mod, b_mod):
    n_layers, d, n = w_mod.shape
    r = cond.shape[0]
    tn = 1024
    return pl.pallas_call(
        _mod_kernel,
        out_shape=jax.ShapeDtypeStruct((n_layers, r, n), F32),
        grid=(n_layers, n // tn),
        in_specs=[
            pl.BlockSpec((r, d), lambda l, j: (0, 0)),
            pl.BlockSpec((None, d, tn), lambda l, j: (l, 0, j)),
            pl.BlockSpec((None, 1, tn), lambda l, j: (l, 0, j)),
        ],
        out_specs=pl.BlockSpec((None, r, tn), lambda l, j: (l, 0, j)),
        compiler_params=_params(2),
        name="modulation",
    )(cond, w_mod, b_mod.reshape(n_layers, 1, n))


def _swiglu_ffn(h, w_in_ref, w_out_ref):
    acc = jnp.zeros((h.shape[0], D_MODEL), F32)
    for c in range(D_FF // FF_CHUNK):
        lo = c * FF_CHUNK
        g = _dot(h, w_in_ref[:, lo:lo + FF_CHUNK])
        u = _dot(h, w_in_ref[:, D_FF + lo:D_FF + lo + FF_CHUNK])
        a = (g * jax.nn.sigmoid(g) * u).astype(BF16)
        acc = acc + _dot(a, w_out_ref[lo:lo + FF_CHUNK, :])
    return acc


def _ffn_kernel(*refs, tm, seq, ctx, n_x_tiles, mod_idx, pre_mix, final_norm):
    it = iter(refs)
    s_ref, m_ref, g_ref, w_in_ref, w_out_ref = (next(it) for _ in range(5))
    if pre_mix:
        fz_ref, gb_ref, u_ref, up_ref, un_ref, att_ref, cw_ref, wo_ref = (next(it) for _ in range(8))
    if final_norm:
        fg_ref = next(it)
    o_ref = next(it)

    s = s_ref[...]
    if pre_mix:
        i = pl.program_id(0)
        u = u_ref[...]
        row = lax.broadcasted_iota(jnp.int32, u.shape, 0)
        u_prev = jnp.where(row == 0, up_ref[SUBLANES - 1:SUBLANES, :], pltpu.roll(u, 1, axis=0))
        u_next = jnp.where(row == tm - 1, un_ref[0:1, :], pltpu.roll(u, tm - 1, axis=0))
        grow = row + i * tm
        len_mask = jnp.where(i >= n_x_tiles, ctx - 1, seq - 1)
        pos = grow & len_mask
        u_prev = jnp.where(pos == 0, 0.0, u_prev)
        u_next = jnp.where(pos == len_mask, 0.0, u_next)
        cv = gb_ref[...] * (cw_ref[0:1, :] * u_prev + cw_ref[1:2, :] * u + cw_ref[2:3, :] * u_next)
        cat = jnp.concatenate([fz_ref[...], cv.astype(BF16), att_ref[...]], axis=-1)
        s = s + m_ref[5:6, :] * _dot(cat, wo_ref[...])

    k = 3 * mod_idx
    h = _rms(s, g_ref[...]) * (1.0 + m_ref[k + 1:k + 2, :]) + m_ref[k:k + 1, :]
    y = _swiglu_ffn(h.astype(BF16), w_in_ref, w_out_ref)
    out = s + 0.5 * m_ref[k + 2:k + 3, :] * y
    if final_norm:
        out = _rms(out, fg_ref[...])
    o_ref[...] = out


def _ffn_call(s, m, g, w_in, w_out, *, geo, mod_idx, mix=None, final_g=None, x_only=False):
    tm, seq, ctx, batch = geo["tm"], geo["seq"], geo["ctx"], geo["batch"]
    t_all = s.shape[0]
    tps = seq // tm
    n_x_tiles = batch * tps
    n_tiles = n_x_tiles if x_only else t_all // tm
    rows8 = t_all // SUBLANES

    row_spec = lambda w: pl.BlockSpec((tm, w), lambda i: (i, 0))
    in_specs = [
        row_spec(D_MODEL),
        pl.BlockSpec((None, N_MOD, D_MODEL), lambda i: (jnp.minimum(i // tps, batch), 0, 0)),
        _const_spec((1, D_MODEL)),
        _const_spec((D_MODEL, 2 * D_FF)),
        _const_spec((D_FF, D_MODEL)),
    ]
    args = [s, m, g.reshape(1, D_MODEL), w_in, w_out]
    if mix is not None:
        fz, gb, u, att, conv_w, w_o = mix
        blk = tm // SUBLANES
        in_specs += [
            row_spec(F_W), row_spec(CONV_W), row_spec(CONV_W),
            pl.BlockSpec((SUBLANES, CONV_W), lambda i: (jnp.maximum(i * blk - 1, 0), 0)),
            pl.BlockSpec((SUBLANES, CONV_W), lambda i: (jnp.minimum((i + 1) * blk, rows8 - 1), 0)),
            row_spec(MLA_OUT),
            _const_spec((3, CONV_W)),
            _const_spec((D_MODEL, D_MODEL)),
        ]
        args += [fz, gb, u, u, u, att, conv_w, w_o]
    if final_g is not None:
        in_specs.append(_const_spec((1, D_MODEL)))
        args.append(final_g.reshape(1, D_MODEL))

    kern = functools.partial(_ffn_kernel, tm=tm, seq=seq, ctx=ctx, n_x_tiles=n_x_tiles, mod_idx=mod_idx,
                             pre_mix=mix is not None, final_norm=final_g is not None)
    return pl.pallas_call(
        kern,
        out_shape=jax.ShapeDtypeStruct((n_tiles * tm, D_MODEL), F32),
        grid=(n_tiles,),
        in_specs=in_specs,
        out_specs=row_spec(D_MODEL),
        compiler_params=_params(1),
        name="ffn_mix" if mix is not None else "ffn",
    )(*args)


def _proj_kernel(s_ref, m_ref, g_ref, w1_ref, qg_ref, wq_ref, kg_ref, wk_ref, wvt_ref, one_ref, cos_ref, sin_ref,
                 bdc_ref, bds_ref, zc_ref, zs_ref, gb_ref, u_ref, k_ref, qt_ref, vt_ref):
    s = s_ref[...]
    h = _rms(s, g_ref[...]) * (1.0 + m_ref[4:5, :]) + m_ref[3:4, :]
    p = _dot(h.astype(BF16), w1_ref[...])

    f = p[:, 0:F_W].astype(BF16)
    zc_ref[...] = _dot(f, bdc_ref[...]).astype(BF16)
    zs_ref[...] = _dot(f, bds_ref[...]).astype(BF16)
    gb_ref[...] = p[:, F_W:F_W + CONV_W]
    u_ref[...] = p[:, F_W + CONV_W:F_W + 2 * CONV_W] * p[:, F_W + 2 * CONV_W:F_W + 3 * CONV_W]

    cos = cos_ref[...]
    sin = sin_ref[...]
    qn = _rms(p[:, OFF_Q:OFF_KV], qg_ref[...]).astype(BF16)
    qq_t = _dot_nt(wq_ref[...], qn)
    kn = _rms(p[:, OFF_KV:OFF_KR], kg_ref[...]).astype(BF16)
    kk = _dot(kn, wk_ref[...])
    kr = p[:, OFF_KR:OFF_KR + HEAD_PAD] * cos + p[:, OFF_KR + HEAD_PAD:OFF_KR + 2 * HEAD_PAD] * sin
    cos_t = cos.T * (ATTN_SCALE * LOG2E)
    sin_t = sin.T * (ATTN_SCALE * LOG2E)
    for hd in range(MLA_HEADS):
        sl = slice(hd * HEAD_PAD, (hd + 1) * HEAD_PAD)
        sl_rot = slice(QK_W + hd * HEAD_PAD, QK_W + (hd + 1) * HEAD_PAD)
        qt_ref[sl, :] = (qq_t[sl, :] * cos_t + qq_t[sl_rot, :] * sin_t).astype(BF16)
        k_ref[:, sl] = (kk[:, sl] + kr).astype(BF16)
    vt_ref[...] = (_dot_nt(wvt_ref[...], kn) + one_ref[...]).astype(BF16)


def _proj_call(s, m, g, w1, qg, wq, kg, wk, wvt, one_col, cos_t, sin_t, bdc, bds, *, geo):
    tm, seq, batch = geo["tm"], geo["seq"], geo["batch"]
    t_all = s.shape[0]
    tps = seq // tm
    n_x_tiles = batch * tps
    row_spec = lambda w: pl.BlockSpec((tm, w), lambda i: (i, 0))
    tab_spec = pl.BlockSpec((tm, HEAD_PAD), lambda i: (jnp.where(i < n_x_tiles, i % tps, tps), 0))
    out_widths = [(F_W, BF16), (F_W, BF16), (CONV_W, F32), (CONV_W, F32), (QK_W, BF16)]
    vt_rows = MLA_HEADS * VT_ROWS
    return pl.pallas_call(
        _proj_kernel,
        out_shape=[jax.ShapeDtypeStruct((t_all, w), dt) for w, dt in out_widths]
        + [jax.ShapeDtypeStruct((QK_W, t_all), BF16), jax.ShapeDtypeStruct((vt_rows, t_all), BF16)],
        grid=(t_all // tm,),
        in_specs=[
            row_spec(D_MODEL),
            pl.BlockSpec((None, N_MOD, D_MODEL), lambda i: (jnp.minimum(i // tps, batch), 0, 0)),
            _const_spec((1, D_MODEL)),
            _const_spec(w1.shape),
            _const_spec((1, Q_LORA)),
            _const_spec(wq.shape),
            _const_spec((1, KV_LORA)),
            _const_spec(wk.shape),
            _const_spec(wvt.shape),
            _const_spec(one_col.shape),
            tab_spec, tab_spec,
            _const_spec(bdc.shape),
            _const_spec(bds.shape),
        ],
        out_specs=[row_spec(w) for w, _ in out_widths]
        + [pl.BlockSpec((QK_W, tm), lambda i: (0, i)), pl.BlockSpec((vt_rows, tm), lambda i: (0, i))],
        compiler_params=_params(1),
        name="in_proj",
    )(s, m, g.reshape(1, D_MODEL), w1, qg.reshape(1, Q_LORA), wq, kg.reshape(1, KV_LORA), wk, wvt, one_col,
      cos_t, sin_t, bdc, bds)


def _attn_heads(qts, key_chunks, lagged):
    heads = range(len(qts))
    n_chunks = len(key_chunks[0])
    mx = [None for _ in heads]
    acc = [None for _ in heads]
    jump = None
    ahead = min(SCORE_LOOKAHEAD, n_chunks)
    scores = [[_dot(key_chunks[h][c][0], qts[h]) for c in range(ahead)] for h in heads]
    for c in range(n_chunks):
        for h in heads:
            s = scores[h].pop(0)
            if c + ahead < n_chunks:
                scores[h].append(_dot(key_chunks[h][c + ahead][0], qts[h]))
            vt = key_chunks[h][c][1]
            cmax = jnp.max(s, axis=0, keepdims=True)
            if c == 0:
                mx[h] = cmax
                acc[h] = _dot(vt, jnp.exp2(s - cmax).astype(BF16))
                continue
            new = jnp.maximum(mx[h], cmax)
            if lagged:
                acc[h] = (acc[h] + _dot(vt, jnp.exp2(s - mx[h]).astype(BF16))) * jnp.exp2(mx[h] - new)
                step = cmax - mx[h]
                jump = step if jump is None else jnp.maximum(jump, step)
            else:
                acc[h] = acc[h] * jnp.exp2(mx[h] - new) + _dot(vt, jnp.exp2(s - new).astype(BF16))
            mx[h] = new
    outs = [a[0:V_DIM, :] / a[V_DIM:V_DIM + 1, :] for a in acc]
    return outs, (None if jump is None else jnp.max(jump))


def _attn_kernel(qt_ref, kx_ref, kc_ref, vtx_ref, vtc_ref, o_ref, *, key_chunk):
    qts, chunks = [], []
    for hd in range(2):
        sl = slice(hd * HEAD_PAD, (hd + 1) * HEAD_PAD)
        vsl = slice(hd * VT_ROWS, (hd + 1) * VT_ROWS)
        qts.append(qt_ref[sl, :])
        chunks.append([(kc_ref[:, sl], vtc_ref[vsl, :])]
                      + [(kx_ref[lo:lo + key_chunk, sl], vtx_ref[vsl, lo:lo + key_chunk])
                         for lo in range(0, kx_ref.shape[0], key_chunk)])
    outs, jump = _attn_heads(qts, chunks, lagged=True)
    o_ref[...] = jnp.concatenate(outs, axis=0).T.astype(BF16)

    @pl.when(jnp.logical_not(jump <= LAG_MAX_JUMP))
    def _():
        exact, _ = _attn_heads(qts, chunks, lagged=False)
        o_ref[...] = jnp.concatenate(exact, axis=0).T.astype(BF16)


def _attn_ctx_kernel(qt_ref, kc_ref, vtc_ref, prev_ref, o_ref):
    del prev_ref
    qts, chunks = [], []
    for hd in range(2):
        sl = slice(hd * HEAD_PAD, (hd + 1) * HEAD_PAD)
        qts.append(qt_ref[sl, :])
        chunks.append([(kc_ref[:, sl], vtc_ref[hd * VT_ROWS:(hd + 1) * VT_ROWS, :])])
    outs, _ = _attn_heads(qts, chunks, lagged=False)
    o_ref[...] = jnp.concatenate(outs, axis=0).T.astype(BF16)


def _attn_call(qt, k, vt, *, geo):
    seq, ctx, batch = geo["seq"], geo["ctx"], geo["batch"]
    tq = min(512, seq)
    nq = seq // tq
    ctx_blk0 = batch * seq // ctx
    return pl.pallas_call(
        functools.partial(_attn_kernel, key_chunk=min(256, seq)),
        out_shape=jax.ShapeDtypeStruct((k.shape[0], MLA_OUT), BF16),
        grid=(batch, MLA_HEADS // 2, nq),
        in_specs=[
            pl.BlockSpec((PAIR_W, tq), lambda b, p, qi: (p, b * nq + qi)),
            pl.BlockSpec((seq, PAIR_W), lambda b, p, qi: (b, p)),
            pl.BlockSpec((ctx, PAIR_W), lambda b, p, qi: (ctx_blk0 + b, p)),
            pl.BlockSpec((2 * VT_ROWS, seq), lambda b, p, qi: (p, b)),
            pl.BlockSpec((2 * VT_ROWS, ctx), lambda b, p, qi: (p, ctx_blk0 + b)),
        ],
        out_specs=pl.BlockSpec((tq, HEAD_PAD), lambda b, p, qi: (b * nq + qi, p)),
        compiler_params=_params(3),
        name="attention",
    )(qt, k, k, vt, vt)


def _attn_ctx_call(qt, k, vt, att, *, geo):
    seq, ctx, batch = geo["seq"], geo["ctx"], geo["batch"]
    ctx_blk0 = batch * seq // ctx
    return pl.pallas_call(
        _attn_ctx_kernel,
        out_shape=jax.ShapeDtypeStruct(att.shape, BF16),
        grid=(batch, MLA_HEADS // 2),
        in_specs=[
            pl.BlockSpec((PAIR_W, ctx), lambda b, p: (p, ctx_blk0 + b)),
            pl.BlockSpec((ctx, PAIR_W), lambda b, p: (ctx_blk0 + b, p)),
            pl.BlockSpec((2 * VT_ROWS, ctx), lambda b, p: (p, ctx_blk0 + b)),
            pl.BlockSpec(memory_space=pl.ANY),
        ],
        out_specs=pl.BlockSpec((ctx, HEAD_PAD), lambda b, p: (ctx_blk0 + b, p)),
        input_output_aliases={3: 0},
        compiler_params=_params(2),
        name="attention_context",
    )(qt, k, vt, att)


def _dft_kernel(c_ref, s_ref, zc_ref, zs_ref, o_ref):
    o_ref[...] = (_dot(c_ref[...], zc_ref[...]) - _dot(s_ref[...], zs_ref[...])).astype(BF16)


def _dft_ctx_kernel(c_ref, s_ref, zc_ref, zs_ref, prev_ref, o_ref):
    del prev_ref
    o_ref[...] = (_dot(c_ref[...], zc_ref[...]) - _dot(s_ref[...], zs_ref[...])).astype(BF16)


def _dft_call(cm, sm, zc, zs, *, geo):
    seq, batch = geo["seq"], geo["batch"]
    tp = min(512, seq)
    npt = seq // tp
    return pl.pallas_call(
        _dft_kernel,
        out_shape=jax.ShapeDtypeStruct((zc.shape[0], F_W), BF16),
        grid=(npt, batch),
        in_specs=[
            pl.BlockSpec((tp, seq), lambda p, b: (p, 0)),
            pl.BlockSpec((tp, seq), lambda p, b: (p, 0)),
            pl.BlockSpec((seq, F_W), lambda p, b: (b, 0)),
            pl.BlockSpec((seq, F_W), lambda p, b: (b, 0)),
        ],
        out_specs=pl.BlockSpec((tp, F_W), lambda p, b: (b * npt + p, 0)),
        compiler_params=_params(2),
        name="dft_latent",
    )(cm, sm, zc, zs)


def _dft_ctx_call(cm, sm, zc, zs, fz, *, geo):
    seq, ctx, batch = geo["seq"], geo["ctx"], geo["batch"]
    blk0 = batch * seq // ctx
    z_spec = pl.BlockSpec((ctx, F_W), lambda b: (blk0 + b, 0))
    return pl.pallas_call(
        _dft_ctx_kernel,
        out_shape=jax.ShapeDtypeStruct(fz.shape, BF16),
        grid=(batch,),
        in_specs=[_const_spec((ctx, ctx)), _const_spec((ctx, ctx)), z_spec, z_spec,
                  pl.BlockSpec(memory_space=pl.ANY)],
        out_specs=z_spec,
        input_output_aliases={4: 0},
        compiler_params=_params(1),
        name="dft_context",
    )(cm, sm, zc, zs, fz)


def _dft_mats(n):
    k = jnp.arange(n, dtype=jnp.int32)[:, None]
    na = max(n // GRID_W, 1)
    nb = n // na
    a = jnp.arange(na, dtype=jnp.int32)[None, :]
    b = jnp.arange(nb, dtype=jnp.int32)[None, :]
    w = 2.0 * np.pi / n
    ang1 = ((k * a * nb) % n).astype(F32) * w
    ang2 = ((k * b) % n).astype(F32) * w
    c1, s1, c2, s2 = jnp.cos(ang1), jnp.sin(ang1), jnp.cos(ang2), jnp.sin(ang2)
    scale = n ** -0.5
    cm = (c1[:, :, None] * c2[:, None, :] - s1[:, :, None] * s2[:, None, :]).reshape(n, n) * scale
    sm = (s1[:, :, None] * c2[:, None, :] + c1[:, :, None] * s2[:, None, :]).reshape(n, n) * scale
    return cm.astype(BF16), sm.astype(BF16)


def _channel_dft_mats():
    j = np.arange(F_GROUP_DIM)
    ang = 2.0 * np.pi * ((j[:, None] * j[None, :]) % F_GROUP_DIM) / F_GROUP_DIM
    eye = np.eye(F_GROUPS)
    scale = F_GROUP_DIM ** -0.5
    bdc = np.kron(eye, np.cos(ang) * scale)
    bds = np.kron(eye, np.sin(ang) * scale)
    return jnp.asarray(bdc, BF16), jnp.asarray(bds, BF16)


def _rope_tables(seq, pad_rows):
    rows = seq // GRID_W
    row = jnp.repeat(jnp.arange(rows), GRID_W).astype(F32)
    col = jnp.tile(jnp.arange(GRID_W), rows).astype(F32)
    inv = 1.0 / (ROPE_BASE ** (jnp.arange(0, AXIS_ROPE, 2, dtype=F32) / AXIS_ROPE))
    ar = row[:, None] * inv
    ac = col[:, None] * inv
    cos32 = jnp.concatenate([jnp.cos(ar), jnp.cos(ar), jnp.cos(ac), jnp.cos(ac)], axis=1)
    sin32 = jnp.concatenate([jnp.sin(ar), jnp.sin(ar), jnp.sin(ac), jnp.sin(ac)], axis=1)
    ones = jnp.ones((seq, QK_NOPE), F32)
    tail = HEAD_PAD - QK_NOPE - QK_ROPE
    cos_t = jnp.concatenate([ones, cos32, jnp.ones((seq, tail), F32)], axis=1)
    sin_t = jnp.concatenate([jnp.zeros((seq, QK_NOPE), F32), sin32, jnp.zeros((seq, tail), F32)], axis=1)
    cos_t = jnp.concatenate([cos_t, jnp.ones((pad_rows, HEAD_PAD), F32)], axis=0)
    sin_t = jnp.concatenate([sin_t, jnp.zeros((pad_rows, HEAD_PAD), F32)], axis=0)
    return cos_t, sin_t


def _rot_half(w):
    h = AXIS_ROPE // 2
    r1, r2, c1, c2 = w[..., 0:h], w[..., h:2 * h], w[..., 2 * h:3 * h], w[..., 3 * h:4 * h]
    return jnp.concatenate([-r2, r1, -c2, c1], axis=-1)


def _layer_weights(w_in, w_uq, w_ukv):
    d = w_in.shape[0]
    kr = w_in[:, OFF_KR:OFF_KR + QK_ROPE]
    pad_l = jnp.zeros((d, QK_NOPE), F32)
    pad_r = jnp.zeros((d, HEAD_PAD - QK_NOPE - QK_ROPE), F32)
    w1 = jnp.concatenate([w_in[:, :OFF_KR], pad_l, kr, pad_r, pad_l, _rot_half(kr), pad_r], axis=1).astype(BF16)

    uq = w_uq.reshape(Q_LORA, MLA_HEADS, QK_NOPE + QK_ROPE)
    zq = jnp.zeros((Q_LORA, MLA_HEADS, HEAD_PAD - QK_NOPE - QK_ROPE), F32)
    wq_a = jnp.concatenate([uq, zq], axis=-1).reshape(Q_LORA, QK_W)
    wq_b = jnp.concatenate([jnp.zeros((Q_LORA, MLA_HEADS, QK_NOPE), F32), _rot_half(uq[..., QK_NOPE:]), zq],
                           axis=-1).reshape(Q_LORA, QK_W)
    wq = jnp.concatenate([wq_a, wq_b], axis=1).T.astype(BF16)

    ukv = w_ukv.reshape(KV_LORA, MLA_HEADS, QK_NOPE + V_DIM)
    wk = jnp.concatenate([ukv[..., :QK_NOPE], jnp.zeros((KV_LORA, MLA_HEADS, HEAD_PAD - QK_NOPE), F32)],
                         axis=-1).reshape(KV_LORA, QK_W)
    wvt = jnp.concatenate([ukv[..., QK_NOPE:], jnp.zeros((KV_LORA, MLA_HEADS, VT_ROWS - V_DIM), F32)],
                          axis=-1).reshape(KV_LORA, MLA_HEADS * VT_ROWS).T
    return w1, wq, wk.astype(BF16), wvt.astype(BF16)


def kernel(x, c, ctx, c_ctx, w_mod, b_mod, ffn1_norm, ffn1_w_in, ffn1_w_out, mix_norm, w_in, conv_w, q_norm, w_uq,
           kv_norm, w_ukv, w_out, ffn2_norm, ffn2_w_in, ffn2_w_out, final_norm):
    batch, seq, d = x.shape
    n_ctx = ctx.shape[1]
    depth = w_mod.shape[0]
    assert d == D_MODEL and seq % GRID_W == 0 and seq % n_ctx == 0 and n_ctx % SUBLANES == 0
    assert seq & (seq - 1) == 0 and n_ctx & (n_ctx - 1) == 0
    tm = 512 if (seq % 512 == 0 and (batch * n_ctx) % 512 == 0) else n_ctx
    geo = dict(tm=tm, seq=seq, ctx=n_ctx, batch=batch)

    s = jnp.concatenate([x.reshape(batch * seq, d), ctx.reshape(batch * n_ctx, d)], axis=0)
    cond = jnp.concatenate([c, c_ctx[None, :]], axis=0)
    mods = _modulation(cond, w_mod, b_mod).reshape(depth, batch + 1, N_MOD, d)

    cos_t, sin_t = _rope_tables(seq, tm)
    cm_x, sm_x = _dft_mats(seq)
    cm_c, sm_c = _dft_mats(n_ctx)
    bdc, bds = _channel_dft_mats()
    one_col = jnp.asarray((np.arange(MLA_HEADS * VT_ROWS) % VT_ROWS >= V_DIM).astype(np.float32)[:, None])

    for l in range(depth):
        last = l == depth - 1
        m = mods[l]
        w1, wq, wk, wvt = _layer_weights(w_in[l], w_uq[l], w_ukv[l])
        s = _ffn_call(s, m, ffn1_norm[l], ffn1_w_in[l].astype(BF16), ffn1_w_out[l].astype(BF16), geo=geo, mod_idx=0)
        zc, zs, gb, u, k, qt, vt = _proj_call(s, m, mix_norm[l], w1, q_norm[l], wq, kv_norm[l], wk, wvt, one_col,
                                              cos_t, sin_t, bdc, bds, geo=geo)
        att = _attn_call(qt, k, vt, geo=geo)
        fz = _dft_call(cm_x, sm_x, zc, zs, geo=geo)
        if not last:
            att = _attn_ctx_call(qt, k, vt, att, geo=geo)
            fz = _dft_ctx_call(cm_c, sm_c, zc, zs, fz, geo=geo)
        s = _ffn_call(s, m, ffn2_norm[l], ffn2_w_in[l].astype(BF16), ffn2_w_out[l].astype(BF16), geo=geo, mod_idx=2,
                      mix=(fz, gb, u, att, conv_w[l], w_out[l].astype(BF16)),
                      final_g=final_norm if last else None, x_only=last)
    return s.reshape(batch, seq, d)
```

```python
import functools

import numpy as np
import jax
import jax.numpy as jnp
from jax import lax
from jax.experimental import pallas as pl
from jax.experimental.pallas import tpu as pltpu

D_MODEL = 1024
D_FF = 2816
N_MOD = 9
EPS = 1e-6
GRID_W = 64
F_W = 256
F_GROUPS = 4
F_GROUP_DIM = 64
CONV_W = 256
MLA_HEADS = 8
QK_NOPE = 64
QK_ROPE = 32
V_DIM = 64
Q_LORA = 384
KV_LORA = 256
MLA_OUT = MLA_HEADS * V_DIM
ROPE_BASE = 10000.0
AXIS_ROPE = QK_ROPE // 2
ATTN_SCALE = (QK_NOPE + QK_ROPE) ** -0.5
OFF_Q = 1024
OFF_KV = OFF_Q + Q_LORA
OFF_KR = OFF_KV + KV_LORA

LANES = 128
SUBLANES = 8
HEAD_PAD = LANES
QK_W = MLA_HEADS * HEAD_PAD
PAIR_W = 2 * HEAD_PAD
VT_ROWS = V_DIM + 16
LOG2E = 1.4426950408889634
SCORE_LOOKAHEAD = 2
LAG_MAX_JUMP = 64.0
FF_CHUNK = 256
VMEM_LIMIT = 56 * 1024 * 1024

BF16 = jnp.bfloat16
F32 = jnp.float32


def _dot(a, b):
    return jnp.dot(a, b, preferred_element_type=F32)


def _dot_nt(a, b):
    return lax.dot_general(a, b, (((1,), (1,)), ((), ())), preferred_element_type=F32)


def _rms(x, g):
    return x * lax.rsqrt(jnp.mean(x * x, axis=-1, keepdims=True) + EPS) * g


def _params(n_axes, flags=None):
    return pltpu.CompilerParams(dimension_semantics=("arbitrary",) * n_axes,
                                vmem_limit_bytes=VMEM_LIMIT, flags=flags)


def _const_spec(shape):
    nd = len(shape)
    return pl.BlockSpec(shape, lambda *_: (0,) * nd, pipeline_mode=pl.Buffered(1))


def _mod_kernel(c_ref, w_ref, b_ref, o_ref):
    c = c_ref[...]
    a = (c * jax.nn.sigmoid(c)).astype(BF16)
    o_ref[...] = _dot(a, w_ref[...].astype(BF16)) + b_ref[...]


def _modulation(cond, w_mod, b_mod):
    n_layers, d, n = w_mod.shape
    r = cond.shape[0]
    tn = 1024
    return pl.pallas_call(
        _mod_kernel,
        out_shape=jax.ShapeDtypeStruct((n_layers, r, n), F32),
        grid=(n_layers, n // tn),
        in_specs=[
            pl.BlockSpec((r, d), lambda l, j: (0, 0)),
            pl.BlockSpec((None, d, tn), lambda l, j: (l, 0, j)),
            pl.BlockSpec((None, 1, tn), lambda l, j: (l, 0, j)),
        ],
        out_specs=pl.BlockSpec((None, r, tn), lambda l, j: (l, 0, j)),
        compiler_params=_params(2),
        name="modulation",
    )(cond, w_mod, b_mod.reshape(n_layers, 1, n))


def _swiglu_ffn(h, w_in_ref, w_out_ref):
    acc = jnp.zeros((h.shape[0], D_MODEL), F32)
    for c in range(D_FF // FF_CHUNK):
        lo = c * FF_CHUNK
        g = _dot(h, w_in_ref[:, lo:lo + FF_CHUNK])
        u = _dot(h, w_in_ref[:, D_FF + lo:D_FF + lo + FF_CHUNK])
        a = (g * jax.nn.sigmoid(g) * u).astype(BF16)
        acc = acc + _dot(a, w_out_ref[lo:lo + FF_CHUNK, :])
    return acc


def _ffn_kernel(*refs, tm, seq, ctx, n_x_tiles, mod_idx, pre_mix, final_norm, split_rows):
    it = iter(refs)
    i = pl.program_id(0)
    is_ctx = i >= n_x_tiles

    def rows(n_split):
        x_ref = next(it)
        if not n_split:
            return x_ref[...]
        c_ref = next(it)
        return jnp.where(is_ctx, c_ref[...], x_ref[...])

    s = rows(split_rows == "stream")
    m_ref, g_ref, w_in_ref, w_out_ref = (next(it) for _ in range(4))
    if pre_mix:
        fz = rows(split_rows == "mix")
        gb_ref, u_ref, up_ref, un_ref = (next(it) for _ in range(4))
        att = rows(split_rows == "mix")
        cw_ref, wo_ref = next(it), next(it)
    if final_norm:
        fg_ref = next(it)
    o_ref = next(it)

    if pre_mix:
        u = u_ref[...]
        row = lax.broadcasted_iota(jnp.int32, u.shape, 0)
        u_prev = jnp.where(row == 0, up_ref[SUBLANES - 1:SUBLANES, :], pltpu.roll(u, 1, axis=0))
        u_next = jnp.where(row == tm - 1, un_ref[0:1, :], pltpu.roll(u, tm - 1, axis=0))
        grow = row + i * tm
        len_mask = jnp.where(is_ctx, ctx - 1, seq - 1)
        pos = grow & len_mask
        u_prev = jnp.where(pos == 0, 0.0, u_prev)
        u_next = jnp.where(pos == len_mask, 0.0, u_next)
        cv = gb_ref[...] * (cw_ref[0:1, :] * u_prev + cw_ref[1:2, :] * u + cw_ref[2:3, :] * u_next)
        cat = jnp.concatenate([fz, cv.astype(BF16), att], axis=-1)
        s = s + m_ref[5:6, :] * _dot(cat, wo_ref[...])

    k = 3 * mod_idx
    h = _rms(s, g_ref[...]) * (1.0 + m_ref[k + 1:k + 2, :]) + m_ref[k:k + 1, :]
    y = _swiglu_ffn(h.astype(BF16), w_in_ref, w_out_ref)
    out = s + 0.5 * m_ref[k + 2:k + 3, :] * y
    if final_norm:
        out = _rms(out, fg_ref[...])
    o_ref[...] = out


def _ffn_call(s, m, g, w_in, w_out, *, geo, mod_idx, mix=None, final_g=None, x_only=False):
    tm, seq, ctx, batch = geo["tm"], geo["seq"], geo["ctx"], geo["batch"]
    t_all = batch * (seq + ctx)
    tps = seq // tm
    n_x_tiles = batch * tps
    n_tiles = n_x_tiles if x_only else t_all // tm
    rows8 = t_all // SUBLANES

    row_spec = lambda w: pl.BlockSpec((tm, w), lambda i: (i, 0))
    in_specs, args = [], []

    def add_rows(a, w):
        if isinstance(a, tuple):
            in_specs.extend([pl.BlockSpec((tm, w), lambda i: (jnp.minimum(i, n_x_tiles - 1), 0)),
                             pl.BlockSpec((tm, w), lambda i: (jnp.maximum(i - n_x_tiles, 0), 0))])
            args.extend(a)
        else:
            in_specs.append(row_spec(w))
            args.append(a)

    add_rows(s, D_MODEL)
    in_specs += [
        pl.BlockSpec((None, N_MOD, D_MODEL), lambda i: (jnp.minimum(i // tps, batch), 0, 0)),
        _const_spec((1, D_MODEL)),
        _const_spec((D_MODEL, 2 * D_FF)),
        _const_spec((D_FF, D_MODEL)),
    ]
    args += [m, g.reshape(1, D_MODEL), w_in, w_out]
    split_rows = "stream" if isinstance(s, tuple) else None
    if mix is not None:
        fz, gb, u, att, conv_w, w_o = mix
        assert isinstance(fz, tuple) == isinstance(att, tuple) and split_rows is None
        split_rows = "mix" if isinstance(fz, tuple) else None
        blk = tm // SUBLANES
        add_rows(fz, F_W)
        in_specs += [
            row_spec(CONV_W), row_spec(CONV_W),
            pl.BlockSpec((SUBLANES, CONV_W), lambda i: (jnp.maximum(i * blk - 1, 0), 0)),
            pl.BlockSpec((SUBLANES, CONV_W), lambda i: (jnp.minimum((i + 1) * blk, rows8 - 1), 0)),
        ]
        args += [gb, u, u, u]
        add_rows(att, MLA_OUT)
        in_specs += [_const_spec((3, CONV_W)), _const_spec((D_MODEL, D_MODEL))]
        args += [conv_w, w_o]
    if final_g is not None:
        in_specs.append(_const_spec((1, D_MODEL)))
        args.append(final_g.reshape(1, D_MODEL))

    kern = functools.partial(_ffn_kernel, tm=tm, seq=seq, ctx=ctx, n_x_tiles=n_x_tiles, mod_idx=mod_idx,
                             pre_mix=mix is not None, final_norm=final_g is not None, split_rows=split_rows)
    return pl.pallas_call(
        kern,
        out_shape=jax.ShapeDtypeStruct((n_tiles * tm, D_MODEL), F32),
        grid=(n_tiles,),
        in_specs=in_specs,
        out_specs=row_spec(D_MODEL),
        compiler_params=_params(1),
        name="ffn_mix" if mix is not None else "ffn",
    )(*args)


def _proj_kernel(s_ref, m_ref, g_ref, w1_ref, qg_ref, wq_ref, kg_ref, wk_ref, wvt_ref, one_ref, cos_ref, sin_ref,
                 bdc_ref, bds_ref, zc_ref, zs_ref, gb_ref, u_ref, k_ref, qt_ref, vt_ref):
    s = s_ref[...]
    h = _rms(s, g_ref[...]) * (1.0 + m_ref[4:5, :]) + m_ref[3:4, :]
    p = _dot(h.astype(BF16), w1_ref[...])

    f = p[:, 0:F_W].astype(BF16)
    zc_ref[...] = _dot(f, bdc_ref[...]).astype(BF16)
    zs_ref[...] = _dot(f, bds_ref[...]).astype(BF16)
    gb_ref[...] = p[:, F_W:F_W + CONV_W]
    u_ref[...] = p[:, F_W + CONV_W:F_W + 2 * CONV_W] * p[:, F_W + 2 * CONV_W:F_W + 3 * CONV_W]

    cos = cos_ref[...]
    sin = sin_ref[...]
    qn = _rms(p[:, OFF_Q:OFF_KV], qg_ref[...]).astype(BF16)
    qq_t = _dot_nt(wq_ref[...], qn)
    kn = _rms(p[:, OFF_KV:OFF_KR], kg_ref[...]).astype(BF16)
    kk = _dot(kn, wk_ref[...])
    kp = p[:, OFF_KR:OFF_KR + HEAD_PAD]
    kr = kp * cos + pltpu.roll(kp, HEAD_PAD - QK_ROPE, axis=1) * sin
    rope = slice(QK_NOPE, QK_NOPE + QK_ROPE)
    cos_t = cos.T * (ATTN_SCALE * LOG2E)
    sin_r = sin.T[rope, :] * (ATTN_SCALE * LOG2E)
    for hd in range(MLA_HEADS):
        sl = slice(hd * HEAD_PAD, (hd + 1) * HEAD_PAD)
        a = qq_t[sl, :] * cos_t
        b = qq_t[QK_W + hd * QK_ROPE:QK_W + (hd + 1) * QK_ROPE, :] * sin_r
        q_h = jnp.concatenate([a[:QK_NOPE], a[rope] + b, a[QK_NOPE + QK_ROPE:]], axis=0)
        qt_ref[sl, :] = q_h.astype(BF16)
        k_ref[:, sl] = (kk[:, sl] + kr).astype(BF16)
    vt_ref[...] = (_dot_nt(wvt_ref[...], kn) + one_ref[...]).astype(BF16)


def _proj_call(s, m, g, w1, qg, wq, kg, wk, wvt, one_col, cos_t, sin_t, bdc, bds, *, geo):
    tm, seq, batch = geo["tm"], geo["seq"], geo["batch"]
    t_all = s.shape[0]
    tps = seq // tm
    n_x_tiles = batch * tps
    row_spec = lambda w: pl.BlockSpec((tm, w), lambda i: (i, 0))
    tab_spec = pl.BlockSpec((tm, HEAD_PAD), lambda i: (jnp.where(i < n_x_tiles, i % tps, tps), 0))
    out_widths = [(F_W, BF16), (F_W, BF16), (CONV_W, F32), (CONV_W, F32), (QK_W, BF16)]
    vt_rows = MLA_HEADS * VT_ROWS
    return pl.pallas_call(
        _proj_kernel,
        out_shape=[jax.ShapeDtypeStruct((t_all, w), dt) for w, dt in out_widths]
        + [jax.ShapeDtypeStruct((QK_W, t_all), BF16), jax.ShapeDtypeStruct((vt_rows, t_all), BF16)],
        grid=(t_all // tm,),
        in_specs=[
            row_spec(D_MODEL),
            pl.BlockSpec((None, N_MOD, D_MODEL), lambda i: (jnp.minimum(i // tps, batch), 0, 0)),
            _const_spec((1, D_MODEL)),
            _const_spec(w1.shape),
            _const_spec((1, Q_LORA)),
            _const_spec(wq.shape),
            _const_spec((1, KV_LORA)),
            _const_spec(wk.shape),
            _const_spec(wvt.shape),
            _const_spec(one_col.shape),
            tab_spec, tab_spec,
            _const_spec(bdc.shape),
            _const_spec(bds.shape),
        ],
        out_specs=[row_spec(w) for w, _ in out_widths]
        + [pl.BlockSpec((QK_W, tm), lambda i: (0, i)), pl.BlockSpec((vt_rows, tm), lambda i: (0, i))],
        compiler_params=_params(1),
        name="in_proj",
    )(s, m, g.reshape(1, D_MODEL), w1, qg.reshape(1, Q_LORA), wq, kg.reshape(1, KV_LORA), wk, wvt, one_col,
      cos_t, sin_t, bdc, bds)


def _attn_heads(qts, key_chunks, lagged):
    heads = range(len(qts))
    n_chunks = len(key_chunks[0])
    mx = [None for _ in heads]
    acc = [None for _ in heads]
    jump = None
    ahead = min(SCORE_LOOKAHEAD, n_chunks)
    scores = [[_dot(key_chunks[h][c][0], qts[h]) for c in range(ahead)] for h in heads]
    for c in range(n_chunks):
        for h in heads:
            s = scores[h].pop(0)
            if c + ahead < n_chunks:
                scores[h].append(_dot(key_chunks[h][c + ahead][0], qts[h]))
            vt = key_chunks[h][c][1]
            cmax = jnp.max(s, axis=0, keepdims=True)
            if c == 0:
                mx[h] = cmax
                acc[h] = _dot(vt, jnp.exp2(s - cmax).astype(BF16))
                continue
            new = jnp.maximum(mx[h], cmax)
            if lagged:
                acc[h] = (acc[h] + _dot(vt, jnp.exp2(s - mx[h]).astype(BF16))) * jnp.exp2(mx[h] - new)
                step = cmax - mx[h]
                jump = step if jump is None else jnp.maximum(jump, step)
            else:
                acc[h] = acc[h] * jnp.exp2(mx[h] - new) + _dot(vt, jnp.exp2(s - new).astype(BF16))
            mx[h] = new
    outs = [a[0:V_DIM, :] / a[V_DIM:V_DIM + 1, :] for a in acc]
    return outs, (None if jump is None else jnp.max(jump))


def _attn_kernel(qt_ref, kx_ref, kc_ref, vtx_ref, vtc_ref, o_ref, *, key_chunk):
    qts, chunks = [], []
    for hd in range(2):
        sl = slice(hd * HEAD_PAD, (hd + 1) * HEAD_PAD)
        vsl = slice(hd * VT_ROWS, (hd + 1) * VT_ROWS)
        qts.append(qt_ref[sl, :])
        chunks.append([(kc_ref[:, sl], vtc_ref[vsl, :])]
                      + [(kx_ref[lo:lo + key_chunk, sl], vtx_ref[vsl, lo:lo + key_chunk])
                         for lo in range(0, kx_ref.shape[0], key_chunk)])
    outs, jump = _attn_heads(qts, chunks, lagged=True)
    o_ref[...] = jnp.concatenate(outs, axis=0).T.astype(BF16)

    @pl.when(jnp.logical_not(jump <= LAG_MAX_JUMP))
    def _():
        exact, _ = _attn_heads(qts, chunks, lagged=False)
        o_ref[...] = jnp.concatenate(exact, axis=0).T.astype(BF16)


def _attn_ctx_kernel(qt_ref, kc_ref, vtc_ref, o_ref):
    qts, chunks = [], []
    for hd in range(2):
        sl = slice(hd * HEAD_PAD, (hd + 1) * HEAD_PAD)
        qts.append(qt_ref[sl, :])
        chunks.append([(kc_ref[:, sl], vtc_ref[hd * VT_ROWS:(hd + 1) * VT_ROWS, :])])
    outs, _ = _attn_heads(qts, chunks, lagged=False)
    o_ref[...] = jnp.concatenate(outs, axis=0).T.astype(BF16)


def _attn_call(qt, k, vt, *, geo):
    seq, ctx, batch = geo["seq"], geo["ctx"], geo["batch"]
    tq = min(512, seq)
    nq = seq // tq
    ctx_blk0 = batch * seq // ctx
    return pl.pallas_call(
        functools.partial(_attn_kernel, key_chunk=min(256, seq)),
        out_shape=jax.ShapeDtypeStruct((batch * seq, MLA_OUT), BF16),
        grid=(batch, MLA_HEADS // 2, nq),
        in_specs=[
            pl.BlockSpec((PAIR_W, tq), lambda b, p, qi: (p, b * nq + qi)),
            pl.BlockSpec((seq, PAIR_W), lambda b, p, qi: (b, p)),
            pl.BlockSpec((ctx, PAIR_W), lambda b, p, qi: (ctx_blk0 + b, p)),
            pl.BlockSpec((2 * VT_ROWS, seq), lambda b, p, qi: (p, b)),
            pl.BlockSpec((2 * VT_ROWS, ctx), lambda b, p, qi: (p, ctx_blk0 + b)),
        ],
        out_specs=pl.BlockSpec((tq, HEAD_PAD), lambda b, p, qi: (b * nq + qi, p)),
        compiler_params=_params(3),
        name="attention",
    )(qt, k, k, vt, vt)


def _attn_ctx_call(qt, k, vt, *, geo):
    seq, ctx, batch = geo["seq"], geo["ctx"], geo["batch"]
    ctx_blk0 = batch * seq // ctx
    return pl.pallas_call(
        _attn_ctx_kernel,
        out_shape=jax.ShapeDtypeStruct((batch * ctx, MLA_OUT), BF16),
        grid=(batch, MLA_HEADS // 2),
        in_specs=[
            pl.BlockSpec((PAIR_W, ctx), lambda b, p: (p, ctx_blk0 + b)),
            pl.BlockSpec((ctx, PAIR_W), lambda b, p: (ctx_blk0 + b, p)),
            pl.BlockSpec((2 * VT_ROWS, ctx), lambda b, p: (p, ctx_blk0 + b)),
        ],
        out_specs=pl.BlockSpec((ctx, HEAD_PAD), lambda b, p: (b, p)),
        compiler_params=_params(2),
        name="attention_context",
    )(qt, k, vt)


def _dft_kernel(c_ref, s_ref, zc_ref, zs_ref, o_ref):
    o_ref[...] = (_dot(c_ref[...], zc_ref[...]) - _dot(s_ref[...], zs_ref[...])).astype(BF16)


def _dft_dense_call(cm, sm, zc, zs, *, n, row0, count):
    tp = min(512, n)
    npt = n // tp
    blk0 = row0 // n
    return pl.pallas_call(
        _dft_kernel,
        out_shape=jax.ShapeDtypeStruct((count * n, F_W), BF16),
        grid=(npt, count),
        in_specs=[
            pl.BlockSpec((tp, n), lambda p, b: (p, 0)),
            pl.BlockSpec((tp, n), lambda p, b: (p, 0)),
            pl.BlockSpec((n, F_W), lambda p, b: (blk0 + b, 0)),
            pl.BlockSpec((n, F_W), lambda p, b: (blk0 + b, 0)),
        ],
        out_specs=pl.BlockSpec((tp, F_W), lambda p, b: (b * npt + p, 0)),
        compiler_params=_params(2),
        name="dft_dense",
    )(cm, sm, zc, zs)


def _dft_mats(n):
    k = jnp.arange(n, dtype=jnp.int32)[:, None]
    na = max(n // GRID_W, 1)
    nb = n // na
    a = jnp.arange(na, dtype=jnp.int32)[None, :]
    b = jnp.arange(nb, dtype=jnp.int32)[None, :]
    w = 2.0 * np.pi / n
    ang1 = ((k * a * nb) % n).astype(F32) * w
    ang2 = ((k * b) % n).astype(F32) * w
    c1, s1, c2, s2 = jnp.cos(ang1), jnp.sin(ang1), jnp.cos(ang2), jnp.sin(ang2)
    scale = n ** -0.5
    cm = (c1[:, :, None] * c2[:, None, :] - s1[:, :, None] * s2[:, None, :]).reshape(n, n) * scale
    sm = (s1[:, :, None] * c2[:, None, :] + c1[:, :, None] * s2[:, None, :]).reshape(n, n) * scale
    return cm.astype(BF16), sm.astype(BF16)


def _channel_dft_mats():
    j = np.arange(F_GROUP_DIM)
    ang = 2.0 * np.pi * ((j[:, None] * j[None, :]) % F_GROUP_DIM) / F_GROUP_DIM
    eye = np.eye(F_GROUPS)
    scale = F_GROUP_DIM ** -0.5
    bdc = np.kron(eye, np.cos(ang) * scale)
    bds = np.kron(eye, np.sin(ang) * scale)
    return jnp.asarray(bdc, BF16), jnp.asarray(bds, BF16)


def _rope_tables(seq, pad_rows):
    rows = seq // GRID_W
    row = jnp.repeat(jnp.arange(rows), GRID_W).astype(F32)
    col = jnp.tile(jnp.arange(GRID_W), rows).astype(F32)
    inv = 1.0 / (ROPE_BASE ** (jnp.arange(0, AXIS_ROPE, 2, dtype=F32) / AXIS_ROPE))
    ar = row[:, None] * inv
    ac = col[:, None] * inv
    cos32 = jnp.concatenate([jnp.cos(ar), jnp.cos(ar), jnp.cos(ac), jnp.cos(ac)], axis=1)
    sin32 = jnp.concatenate([jnp.sin(ar), jnp.sin(ar), jnp.sin(ac), jnp.sin(ac)], axis=1)
    ones = jnp.ones((seq, QK_NOPE), F32)
    tail = HEAD_PAD - QK_NOPE - QK_ROPE
    cos_t = jnp.concatenate([ones, cos32, jnp.zeros((seq, tail), F32)], axis=1)
    sin_t = jnp.concatenate([jnp.zeros((seq, QK_NOPE), F32), sin32, jnp.zeros((seq, tail), F32)], axis=1)
    ident = jnp.concatenate([jnp.ones((pad_rows, QK_NOPE + QK_ROPE), F32), jnp.zeros((pad_rows, tail), F32)], axis=1)
    cos_t = jnp.concatenate([cos_t, ident], axis=0)
    sin_t = jnp.concatenate([sin_t, jnp.zeros((pad_rows, HEAD_PAD), F32)], axis=0)
    return cos_t, sin_t


def _rot_half(w):
    h = AXIS_ROPE // 2
    r1, r2, c1, c2 = w[..., 0:h], w[..., h:2 * h], w[..., 2 * h:3 * h], w[..., 3 * h:4 * h]
    return jnp.concatenate([-r2, r1, -c2, c1], axis=-1)


def _layer_weights(w_in, w_uq, w_ukv):
    d = w_in.shape[0]
    kr = w_in[:, OFF_KR:OFF_KR + QK_ROPE]
    w1 = jnp.concatenate([w_in[:, :OFF_KR], jnp.zeros((d, QK_NOPE), F32), kr, _rot_half(kr)], axis=1).astype(BF16)

    uq = w_uq.reshape(Q_LORA, MLA_HEADS, QK_NOPE + QK_ROPE)
    zq = jnp.zeros((Q_LORA, MLA_HEADS, HEAD_PAD - QK_NOPE - QK_ROPE), F32)
    wq_a = jnp.concatenate([uq, zq], axis=-1).reshape(Q_LORA, QK_W)
    wq_b = _rot_half(uq[..., QK_NOPE:]).reshape(Q_LORA, MLA_HEADS * QK_ROPE)
    wq = jnp.concatenate([wq_a, wq_b], axis=1).T.astype(BF16)

    ukv = w_ukv.reshape(KV_LORA, MLA_HEADS, QK_NOPE + V_DIM)
    wk = jnp.concatenate([ukv[..., :QK_NOPE], jnp.zeros((KV_LORA, MLA_HEADS, HEAD_PAD - QK_NOPE), F32)],
                         axis=-1).reshape(KV_LORA, QK_W)
    wvt = jnp.concatenate([ukv[..., QK_NOPE:], jnp.zeros((KV_LORA, MLA_HEADS, VT_ROWS - V_DIM), F32)],
                          axis=-1).reshape(KV_LORA, MLA_HEADS * VT_ROWS).T
    return w1, wq, wk.astype(BF16), wvt.astype(BF16)


def kernel(x, c, ctx, c_ctx, w_mod, b_mod, ffn1_norm, ffn1_w_in, ffn1_w_out, mix_norm, w_in, conv_w, q_norm, w_uq,
           kv_norm, w_ukv, w_out, ffn2_norm, ffn2_w_in, ffn2_w_out, final_norm):
    batch, seq, d = x.shape
    n_ctx = ctx.shape[1]
    depth = w_mod.shape[0]
    assert d == D_MODEL and seq % GRID_W == 0 and seq % n_ctx == 0 and n_ctx % SUBLANES == 0
    assert seq & (seq - 1) == 0 and n_ctx & (n_ctx - 1) == 0
    tm = 512 if (seq % 512 == 0 and (batch * n_ctx) % 512 == 0) else n_ctx
    geo = dict(tm=tm, seq=seq, ctx=n_ctx, batch=batch)

    s = (x.reshape(batch * seq, d), ctx.reshape(batch * n_ctx, d))
    cond = jnp.concatenate([c, c_ctx[None, :]], axis=0)
    mods = _modulation(cond, w_mod, b_mod).reshape(depth, batch + 1, N_MOD, d)

    cos_t, sin_t = _rope_tables(seq, tm)
    cm_x, sm_x = _dft_mats(seq)
    cm_c, sm_c = _dft_mats(n_ctx)
    bdc, bds = _channel_dft_mats()
    one_col = jnp.asarray((np.arange(MLA_HEADS * VT_ROWS) % VT_ROWS >= V_DIM).astype(np.float32)[:, None])

    for l in range(depth):
        last = l == depth - 1
        m = mods[l]
        w1, wq, wk, wvt = _layer_weights(w_in[l], w_uq[l], w_ukv[l])
        s = _ffn_call(s, m, ffn1_norm[l], ffn1_w_in[l].astype(BF16), ffn1_w_out[l].astype(BF16), geo=geo, mod_idx=0)
        zc, zs, gb, u, k, qt, vt = _proj_call(s, m, mix_norm[l], w1, q_norm[l], wq, kv_norm[l], wk, wvt, one_col,
                                              cos_t, sin_t, bdc, bds, geo=geo)
        att = _attn_call(qt, k, vt, geo=geo)
        fz = _dft_dense_call(cm_x, sm_x, zc, zs, n=seq, row0=0, count=batch)
        if not last:
            att = (att, _attn_ctx_call(qt, k, vt, geo=geo))
            fz = (fz, _dft_dense_call(cm_c, sm_c, zc, zs, n=n_ctx, row0=batch * seq, count=batch))
        s = _ffn_call(s, m, ffn2_norm[l], ffn2_w_in[l].astype(BF16), ffn2_w_out[l].astype(BF16), geo=geo, mod_idx=2,
                      mix=(fz, gb, u, att, conv_w[l], w_out[l].astype(BF16)),
                      final_g=final_norm if last else None, x_only=last)
    return s.reshape(batch, seq, d)
```

```python
import functools

import numpy as np
import jax
import jax.numpy as jnp
from jax import lax
from jax.experimental import pallas as pl
from jax.experimental.pallas import tpu as pltpu

D_MODEL = 1024
D_FF = 2816
N_MOD = 9
EPS = 1e-6
GRID_W = 64
F_W = 256
F_GROUPS = 4
F_GROUP_DIM = 64
CONV_W = 256
MLA_HEADS = 8
QK_NOPE = 64
QK_ROPE = 32
V_DIM = 64
Q_LORA = 384
KV_LORA = 256
MLA_OUT = MLA_HEADS * V_DIM
ROPE_BASE = 10000.0
AXIS_ROPE = QK_ROPE // 2
ATTN_SCALE = (QK_NOPE + QK_ROPE) ** -0.5
OFF_Q = 1024
OFF_KV = OFF_Q + Q_LORA
OFF_KR = OFF_KV + KV_LORA

LANES = 128
SUBLANES = 8
HEAD_PAD = LANES
QK_W = MLA_HEADS * HEAD_PAD
PAIR_W = 2 * HEAD_PAD
VT_ROWS = V_DIM + 16
LOG2E = 1.4426950408889634
ATTN_HEADS_PER_STEP = 4
SCORE_LOOKAHEAD = 1
LAG_MAX_JUMP = 64.0
FF_CHUNK = 256
VMEM_LIMIT = 56 * 1024 * 1024

BF16 = jnp.bfloat16
F32 = jnp.float32


def _dot(a, b):
    return jnp.dot(a, b, preferred_element_type=F32)


def _dot_nt(a, b):
    return lax.dot_general(a, b, (((1,), (1,)), ((), ())), preferred_element_type=F32)


def _rms(x, g):
    return x * lax.rsqrt(jnp.mean(x * x, axis=-1, keepdims=True) + EPS) * g


def _params(n_axes, flags=None):
    return pltpu.CompilerParams(dimension_semantics=("arbitrary",) * n_axes,
                                vmem_limit_bytes=VMEM_LIMIT, flags=flags)


def _const_spec(shape):
    nd = len(shape)
    return pl.BlockSpec(shape, lambda *_: (0,) * nd, pipeline_mode=pl.Buffered(1))


def _mod_kernel(c_ref, w_ref, b_ref, o_ref):
    c = c_ref[...]
    a = (c * jax.nn.sigmoid(c)).astype(BF16)
    o_ref[...] = _dot(a, w_ref[...].astype(BF16)) + b_ref[...]


def _modulation(cond, w_mod, b_mod):
    n_layers, d, n = w_mod.shape
    r = cond.shape[0]
    tn = 1024
    return pl.pallas_call(
        _mod_kernel,
        out_shape=jax.ShapeDtypeStruct((n_layers, r, n), F32),
        grid=(n_layers, n // tn),
        in_specs=[
            pl.BlockSpec((r, d), lambda l, j: (0, 0)),
            pl.BlockSpec((None, d, tn), lambda l, j: (l, 0, j)),
            pl.BlockSpec((None, 1, tn), lambda l, j: (l, 0, j)),
        ],
        out_specs=pl.BlockSpec((None, r, tn), lambda l, j: (l, 0, j)),
        compiler_params=_params(2),
        name="modulation",
    )(cond, w_mod, b_mod.reshape(n_layers, 1, n))


def _swiglu_ffn(h, w_in_ref, w_out_ref):
    acc = jnp.zeros((h.shape[0], D_MODEL), F32)
    for c in range(D_FF // FF_CHUNK):
        lo = c * FF_CHUNK
        g = _dot(h, w_in_ref[:, lo:lo + FF_CHUNK])
        u = _dot(h, w_in_ref[:, D_FF + lo:D_FF + lo + FF_CHUNK])
        a = (g * jax.nn.sigmoid(g) * u).astype(BF16)
        acc = acc + _dot(a, w_out_ref[lo:lo + FF_CHUNK, :])
    return acc


def _ffn_kernel(*refs, tm, seq, ctx, n_x_tiles, mod_idx, pre_mix, final_norm, split_rows):
    it = iter(refs)
    i = pl.program_id(0)
    is_ctx = i >= n_x_tiles

    def rows(n_split):
        x_ref = next(it)
        if not n_split:
            return x_ref[...]
        c_ref = next(it)
        return jnp.where(is_ctx, c_ref[...], x_ref[...])

    s = rows(split_rows == "stream")
    m_ref, g_ref, w_in_ref, w_out_ref = (next(it) for _ in range(4))
    if pre_mix:
        fz = rows(split_rows == "mix")
        gb_ref, u_ref, up_ref, un_ref = (next(it) for _ in range(4))
        att = rows(split_rows == "mix")
        cw_ref, wo_ref = next(it), next(it)
    if final_norm:
        fg_ref = next(it)
    o_ref = next(it)

    if pre_mix:
        u = u_ref[...]
        row = lax.broadcasted_iota(jnp.int32, u.shape, 0)
        u_prev = jnp.where(row == 0, up_ref[SUBLANES - 1:SUBLANES, :], pltpu.roll(u, 1, axis=0))
        u_next = jnp.where(row == tm - 1, un_ref[0:1, :], pltpu.roll(u, tm - 1, axis=0))
        grow = row + i * tm
        len_mask = jnp.where(is_ctx, ctx - 1, seq - 1)
        pos = grow & len_mask
        u_prev = jnp.where(pos == 0, 0.0, u_prev)
        u_next = jnp.where(pos == len_mask, 0.0, u_next)
        cv = gb_ref[...] * (cw_ref[0:1, :] * u_prev + cw_ref[1:2, :] * u + cw_ref[2:3, :] * u_next)
        cat = jnp.concatenate([fz, cv.astype(BF16), att], axis=-1)
        s = s + m_ref[5:6, :] * _dot(cat, wo_ref[...])

    k = 3 * mod_idx
    h = _rms(s, g_ref[...]) * (1.0 + m_ref[k + 1:k + 2, :]) + m_ref[k:k + 1, :]
    y = _swiglu_ffn(h.astype(BF16), w_in_ref, w_out_ref)
    out = s + 0.5 * m_ref[k + 2:k + 3, :] * y
    if final_norm:
        out = _rms(out, fg_ref[...])
    o_ref[...] = out


def _ffn_call(s, m, g, w_in, w_out, *, geo, mod_idx, mix=None, final_g=None, x_only=False):
    tm, seq, ctx, batch = geo["tm"], geo["seq"], geo["ctx"], geo["batch"]
    t_all = batch * (seq + ctx)
    tps = seq // tm
    n_x_tiles = batch * tps
    n_tiles = n_x_tiles if x_only else t_all // tm
    rows8 = t_all // SUBLANES

    row_spec = lambda w: pl.BlockSpec((tm, w), lambda i: (i, 0))
    in_specs, args = [], []

    def add_rows(a, w):
        if isinstance(a, tuple):
            in_specs.extend([pl.BlockSpec((tm, w), lambda i: (jnp.minimum(i, n_x_tiles - 1), 0)),
                             pl.BlockSpec((tm, w), lambda i: (jnp.maximum(i - n_x_tiles, 0), 0))])
            args.extend(a)
        else:
            in_specs.append(row_spec(w))
            args.append(a)

    add_rows(s, D_MODEL)
    in_specs += [
        pl.BlockSpec((None, N_MOD, D_MODEL), lambda i: (jnp.minimum(i // tps, batch), 0, 0)),
        _const_spec((1, D_MODEL)),
        _const_spec((D_MODEL, 2 * D_FF)),
        _const_spec((D_FF, D_MODEL)),
    ]
    args += [m, g.reshape(1, D_MODEL), w_in, w_out]
    split_rows = "stream" if isinstance(s, tuple) else None
    if mix is not None:
        fz, gb, u, att, conv_w, w_o = mix
        assert isinstance(fz, tuple) == isinstance(att, tuple) and split_rows is None
        split_rows = "mix" if isinstance(fz, tuple) else None
        blk = tm // SUBLANES
        add_rows(fz, F_W)
        in_specs += [
            row_spec(CONV_W), row_spec(CONV_W),
            pl.BlockSpec((SUBLANES, CONV_W), lambda i: (jnp.maximum(i * blk - 1, 0), 0)),
            pl.BlockSpec((SUBLANES, CONV_W), lambda i: (jnp.minimum((i + 1) * blk, rows8 - 1), 0)),
        ]
        args += [gb, u, u, u]
        add_rows(att, MLA_OUT)
        in_specs += [_const_spec((3, CONV_W)), _const_spec((D_MODEL, D_MODEL))]
        args += [conv_w, w_o]
    if final_g is not None:
        in_specs.append(_const_spec((1, D_MODEL)))
        args.append(final_g.reshape(1, D_MODEL))

    kern = functools.partial(_ffn_kernel, tm=tm, seq=seq, ctx=ctx, n_x_tiles=n_x_tiles, mod_idx=mod_idx,
                             pre_mix=mix is not None, final_norm=final_g is not None, split_rows=split_rows)
    return pl.pallas_call(
        kern,
        out_shape=jax.ShapeDtypeStruct((n_tiles * tm, D_MODEL), F32),
        grid=(n_tiles,),
        in_specs=in_specs,
        out_specs=row_spec(D_MODEL),
        compiler_params=_params(1),
        name="ffn_mix" if mix is not None else "ffn",
    )(*args)


def _proj_kernel(s_ref, m_ref, g_ref, w1_ref, qg_ref, wq_ref, kg_ref, wk_ref, wvt_ref, one_ref, cos_ref, sin_ref,
                 bdc_ref, bds_ref, zc_ref, zs_ref, gb_ref, u_ref, k_ref, qt_ref, vt_ref):
    s = s_ref[...]
    h = _rms(s, g_ref[...]) * (1.0 + m_ref[4:5, :]) + m_ref[3:4, :]
    p = _dot(h.astype(BF16), w1_ref[...])

    f = p[:, 0:F_W].astype(BF16)
    zc_ref[...] = _dot(f, bdc_ref[...]).astype(BF16)
    zs_ref[...] = _dot(f, bds_ref[...]).astype(BF16)
    gb_ref[...] = p[:, F_W:F_W + CONV_W]
    u_ref[...] = p[:, F_W + CONV_W:F_W + 2 * CONV_W] * p[:, F_W + 2 * CONV_W:F_W + 3 * CONV_W]

    cos = cos_ref[...]
    sin = sin_ref[...]
    qn = _rms(p[:, OFF_Q:OFF_KV], qg_ref[...]).astype(BF16)
    qq_t = _dot_nt(wq_ref[...], qn)
    kn = _rms(p[:, OFF_KV:OFF_KR], kg_ref[...]).astype(BF16)
    kk = _dot(kn, wk_ref[...])
    kp = p[:, OFF_KR:OFF_KR + HEAD_PAD]
    kr = kp * cos + pltpu.roll(kp, HEAD_PAD - QK_ROPE, axis=1) * sin
    rope = slice(QK_NOPE, QK_NOPE + QK_ROPE)
    cos_t = cos.T * (ATTN_SCALE * LOG2E)
    sin_r = sin.T[rope, :] * (ATTN_SCALE * LOG2E)
    for hd in range(MLA_HEADS):
        sl = slice(hd * HEAD_PAD, (hd + 1) * HEAD_PAD)
        a = qq_t[sl, :] * cos_t
        b = qq_t[QK_W + hd * QK_ROPE:QK_W + (hd + 1) * QK_ROPE, :] * sin_r
        q_h = jnp.concatenate([a[:QK_NOPE], a[rope] + b, a[QK_NOPE + QK_ROPE:]], axis=0)
        qt_ref[sl, :] = q_h.astype(BF16)
        k_ref[:, sl] = (kk[:, sl] + kr).astype(BF16)
    vt_ref[...] = (_dot_nt(wvt_ref[...], kn) + one_ref[...]).astype(BF16)


def _proj_call(s, m, g, w1, qg, wq, kg, wk, wvt, one_col, cos_t, sin_t, bdc, bds, *, geo):
    tm, seq, batch = geo["tm"], geo["seq"], geo["batch"]
    t_all = s.shape[0]
    tps = seq // tm
    n_x_tiles = batch * tps
    row_spec = lambda w: pl.BlockSpec((tm, w), lambda i: (i, 0))
    tab_spec = pl.BlockSpec((tm, HEAD_PAD), lambda i: (jnp.where(i < n_x_tiles, i % tps, tps), 0))
    out_widths = [(F_W, BF16), (F_W, BF16), (CONV_W, F32), (CONV_W, F32), (QK_W, BF16)]
    vt_rows = MLA_HEADS * VT_ROWS
    return pl.pallas_call(
        _proj_kernel,
        out_shape=[jax.ShapeDtypeStruct((t_all, w), dt) for w, dt in out_widths]
        + [jax.ShapeDtypeStruct((QK_W, t_all), BF16), jax.ShapeDtypeStruct((vt_rows, t_all), BF16)],
        grid=(t_all // tm,),
        in_specs=[
            row_spec(D_MODEL),
            pl.BlockSpec((None, N_MOD, D_MODEL), lambda i: (jnp.minimum(i // tps, batch), 0, 0)),
            _const_spec((1, D_MODEL)),
            _const_spec(w1.shape),
            _const_spec((1, Q_LORA)),
            _const_spec(wq.shape),
            _const_spec((1, KV_LORA)),
            _const_spec(wk.shape),
            _const_spec(wvt.shape),
            _const_spec(one_col.shape),
            tab_spec, tab_spec,
            _const_spec(bdc.shape),
            _const_spec(bds.shape),
        ],
        out_specs=[row_spec(w) for w, _ in out_widths]
        + [pl.BlockSpec((QK_W, tm), lambda i: (0, i)), pl.BlockSpec((vt_rows, tm), lambda i: (0, i))],
        compiler_params=_params(1),
        name="in_proj",
    )(s, m, g.reshape(1, D_MODEL), w1, qg.reshape(1, Q_LORA), wq, kg.reshape(1, KV_LORA), wk, wvt, one_col,
      cos_t, sin_t, bdc, bds)


def _attn_heads(qts, key_chunks, lagged):
    heads = range(len(qts))
    n_chunks = len(key_chunks[0])
    mx = [None for _ in heads]
    acc = [None for _ in heads]
    jump = None
    ahead = min(SCORE_LOOKAHEAD, n_chunks)
    scores = [[_dot(key_chunks[h][c][0], qts[h]) for c in range(ahead)] for h in heads]
    for c in range(n_chunks):
        for h in heads:
            if c + ahead < n_chunks:
                scores[h].append(_dot(key_chunks[h][c + ahead][0], qts[h]))
            s = scores[h].pop(0)
            vt = key_chunks[h][c][1]
            cmax = jnp.max(s, axis=0, keepdims=True)
            if c == 0:
                mx[h] = cmax
                acc[h] = _dot(vt, jnp.exp2(s - cmax).astype(BF16))
                continue
            new = jnp.maximum(mx[h], cmax)
            if lagged:
                acc[h] = (acc[h] + _dot(vt, jnp.exp2(s - mx[h]).astype(BF16))) * jnp.exp2(mx[h] - new)
                step = cmax - mx[h]
                jump = step if jump is None else jnp.maximum(jump, step)
            else:
                acc[h] = acc[h] * jnp.exp2(mx[h] - new) + _dot(vt, jnp.exp2(s - new).astype(BF16))
            mx[h] = new
    outs = [a[0:V_DIM, :] / a[V_DIM:V_DIM + 1, :] for a in acc]
    return outs, (None if jump is None else jnp.max(jump))


def _attn_kernel(qt_ref, kx_ref, kc_ref, vtx_ref, vtc_ref, o_ref, *, key_chunk):
    qts, chunks = [], []
    for hd in range(qt_ref.shape[0] // HEAD_PAD):
        sl = slice(hd * HEAD_PAD, (hd + 1) * HEAD_PAD)
        vsl = slice(hd * VT_ROWS, (hd + 1) * VT_ROWS)
        qts.append(qt_ref[sl, :])
        chunks.append([(kc_ref[:, sl], vtc_ref[vsl, :])]
                      + [(kx_ref[lo:lo + key_chunk, sl], vtx_ref[vsl, lo:lo + key_chunk])
                         for lo in range(0, kx_ref.shape[0], key_chunk)])
    outs, jump = _attn_heads(qts, chunks, lagged=True)
    o_ref[...] = jnp.concatenate(outs, axis=0).T.astype(BF16)

    @pl.when(jnp.logical_not(jump <= LAG_MAX_JUMP))
    def _():
        exact, _ = _attn_heads(qts, chunks, lagged=False)
        o_ref[...] = jnp.concatenate(exact, axis=0).T.astype(BF16)


def _attn_ctx_kernel(qt_ref, kc_ref, vtc_ref, o_ref):
    qts, chunks = [], []
    for hd in range(2):
        sl = slice(hd * HEAD_PAD, (hd + 1) * HEAD_PAD)
        qts.append(qt_ref[sl, :])
        chunks.append([(kc_ref[:, sl], vtc_ref[hd * VT_ROWS:(hd + 1) * VT_ROWS, :])])
    outs, _ = _attn_heads(qts, chunks, lagged=False)
    o_ref[...] = jnp.concatenate(outs, axis=0).T.astype(BF16)


def _attn_call(qt, k, vt, *, geo):
    seq, ctx, batch = geo["seq"], geo["ctx"], geo["batch"]
    tq = min(512, seq)
    nq = seq // tq
    ctx_blk0 = batch * seq // ctx
    hps = ATTN_HEADS_PER_STEP
    return pl.pallas_call(
        functools.partial(_attn_kernel, key_chunk=min(256, seq)),
        out_shape=jax.ShapeDtypeStruct((batch * seq, MLA_OUT), BF16),
        grid=(batch, MLA_HEADS // hps, nq),
        in_specs=[
            pl.BlockSpec((hps * HEAD_PAD, tq), lambda b, p, qi: (p, b * nq + qi)),
            pl.BlockSpec((seq, hps * HEAD_PAD), lambda b, p, qi: (b, p)),
            pl.BlockSpec((ctx, hps * HEAD_PAD), lambda b, p, qi: (ctx_blk0 + b, p)),
            pl.BlockSpec((hps * VT_ROWS, seq), lambda b, p, qi: (p, b)),
            pl.BlockSpec((hps * VT_ROWS, ctx), lambda b, p, qi: (p, ctx_blk0 + b)),
        ],
        out_specs=pl.BlockSpec((tq, hps * V_DIM), lambda b, p, qi: (b * nq + qi, p)),
        compiler_params=_params(3),
        name="attention",
    )(qt, k, k, vt, vt)


def _attn_ctx_call(qt, k, vt, *, geo):
    seq, ctx, batch = geo["seq"], geo["ctx"], geo["batch"]
    ctx_blk0 = batch * seq // ctx
    return pl.pallas_call(
        _attn_ctx_kernel,
        out_shape=jax.ShapeDtypeStruct((batch * ctx, MLA_OUT), BF16),
        grid=(batch, MLA_HEADS // 2),
        in_specs=[
            pl.BlockSpec((PAIR_W, ctx), lambda b, p: (p, ctx_blk0 + b)),
            pl.BlockSpec((ctx, PAIR_W), lambda b, p: (ctx_blk0 + b, p)),
            pl.BlockSpec((2 * VT_ROWS, ctx), lambda b, p: (p, ctx_blk0 + b)),
        ],
        out_specs=pl.BlockSpec((ctx, HEAD_PAD), lambda b, p: (b, p)),
        compiler_params=_params(2),
        name="attention_context",
    )(qt, k, vt)


def _dft_kernel(c_ref, s_ref, zc_ref, zs_ref, o_ref):
    o_ref[...] = (_dot(c_ref[...], zc_ref[...]) - _dot(s_ref[...], zs_ref[...])).astype(BF16)


def _dft_dense_call(cm, sm, zc, zs, *, n, row0, count):
    tp = min(512, n)
    npt = n // tp
    blk0 = row0 // n
    return pl.pallas_call(
        _dft_kernel,
        out_shape=jax.ShapeDtypeStruct((count * n, F_W), BF16),
        grid=(npt, count),
        in_specs=[
            pl.BlockSpec((tp, n), lambda p, b: (p, 0)),
            pl.BlockSpec((tp, n), lambda p, b: (p, 0)),
            pl.BlockSpec((n, F_W), lambda p, b: (blk0 + b, 0)),
            pl.BlockSpec((n, F_W), lambda p, b: (blk0 + b, 0)),
        ],
        out_specs=pl.BlockSpec((tp, F_W), lambda p, b: (b * npt + p, 0)),
        compiler_params=_params(2),
        name="dft_dense",
    )(cm, sm, zc, zs)


def _flip_rows(flip_ref, src_ref, lo, hi, ft):
    return jnp.concatenate([_dot(flip_ref[...], src_ref[hi - (t + 1) * ft:hi - t * ft, :])
                            for t in range((hi - lo) // ft)], axis=0)


def _dft_folded_kernel(c_ref, s_ref, zc_ref, zs_ref, flip_ref, o_ref, ec_ref, os_ref, g_ref, mid_ref, *, tp, ft):
    n = zc_ref.shape[0]
    half = n // 2
    p = pl.program_id(1)
    scale = n ** -0.5

    @pl.when(p == 0)
    def _():
        row = lax.broadcasted_iota(jnp.int32, (half, F_W), 0)
        for z_ref, dst_ref, sign in ((zc_ref, ec_ref, 1.0), (zs_ref, os_ref, -1.0)):
            rev = _flip_rows(flip_ref, z_ref, half, n, ft)
            mirror = jnp.where(row == 0, 0.0, pltpu.roll(rev, 1, axis=0))
            folded = z_ref[0:half, :].astype(F32) + sign * mirror
            dst_ref[...] = folded.astype(BF16)
            if z_ref is zc_ref:
                alt_sum = jnp.sum(jnp.where((row & 1) == 0, folded, -folded), axis=0, keepdims=True)
                mid_ref[0:1, :] = (alt_sum + zc_ref[half:half + 1, :].astype(F32)) * scale

    rows = pl.ds(pl.multiple_of(p * tp, tp), tp)
    e = _dot(c_ref[rows, :], ec_ref[...])
    o = _dot(s_ref[rows, :], os_ref[...])
    k = p * tp + lax.broadcasted_iota(jnp.int32, (tp, F_W), 0)
    e = e + jnp.where((k & 1) == 0, scale, -scale) * zc_ref[half:half + 1, :].astype(F32)
    o_ref[rows, :] = (e - o).astype(BF16)
    g_ref[rows, :] = (e + o).astype(BF16)

    @pl.when(p == pl.num_programs(1) - 1)
    def _():
        row = lax.broadcasted_iota(jnp.int32, (half, F_W), 0)
        rev = _flip_rows(flip_ref, g_ref, 0, half, ft)
        upper = jnp.where(row == 0, mid_ref[0:1, :], pltpu.roll(rev, 1, axis=0))
        o_ref[half:n, :] = upper.astype(BF16)


def _dft_folded_call(cm, sm, zc, zs, *, n, count):
    half = n // 2
    tp = min(512, half)
    ft = min(512, half)
    flip = jnp.asarray(np.eye(ft)[::-1], BF16)
    seq_spec = pl.BlockSpec((n, F_W), lambda b, p: (b, 0))
    return pl.pallas_call(
        functools.partial(_dft_folded_kernel, tp=tp, ft=ft),
        out_shape=jax.ShapeDtypeStruct((count * n, F_W), BF16),
        grid=(count, half // tp),
        in_specs=[_const_spec((half, half)), _const_spec((half, half)), seq_spec, seq_spec, _const_spec((ft, ft))],
        out_specs=seq_spec,
        scratch_shapes=[pltpu.VMEM((half, F_W), BF16), pltpu.VMEM((half, F_W), BF16), pltpu.VMEM((half, F_W), BF16),
                        pltpu.VMEM((SUBLANES, F_W), F32)],
        compiler_params=_params(2),
        name="dft_folded",
    )(cm, sm, zc, zs, flip)


def _dft_mats(n, size=None):
    rows = cols = n if size is None else size
    k = jnp.arange(rows, dtype=jnp.int32)[:, None]
    nb = min(GRID_W, cols)
    na = cols // nb
    a = jnp.arange(na, dtype=jnp.int32)[None, :]
    b = jnp.arange(nb, dtype=jnp.int32)[None, :]
    w = 2.0 * np.pi / n
    ang1 = ((k * a * nb) % n).astype(F32) * w
    ang2 = ((k * b) % n).astype(F32) * w
    c1, s1, c2, s2 = jnp.cos(ang1), jnp.sin(ang1), jnp.cos(ang2), jnp.sin(ang2)
    scale = n ** -0.5
    cm = (c1[:, :, None] * c2[:, None, :] - s1[:, :, None] * s2[:, None, :]).reshape(rows, cols) * scale
    sm = (s1[:, :, None] * c2[:, None, :] + c1[:, :, None] * s2[:, None, :]).reshape(rows, cols) * scale
    return cm.astype(BF16), sm.astype(BF16)


def _channel_dft_mats():
    j = np.arange(F_GROUP_DIM)
    ang = 2.0 * np.pi * ((j[:, None] * j[None, :]) % F_GROUP_DIM) / F_GROUP_DIM
    eye = np.eye(F_GROUPS)
    scale = F_GROUP_DIM ** -0.5
    bdc = np.kron(eye, np.cos(ang) * scale)
    bds = np.kron(eye, np.sin(ang) * scale)
    return jnp.asarray(bdc, BF16), jnp.asarray(bds, BF16)


def _rope_tables(seq, pad_rows):
    rows = seq // GRID_W
    row = jnp.repeat(jnp.arange(rows), GRID_W).astype(F32)
    col = jnp.tile(jnp.arange(GRID_W), rows).astype(F32)
    inv = 1.0 / (ROPE_BASE ** (jnp.arange(0, AXIS_ROPE, 2, dtype=F32) / AXIS_ROPE))
    ar = row[:, None] * inv
    ac = col[:, None] * inv
    cos32 = jnp.concatenate([jnp.cos(ar), jnp.cos(ar), jnp.cos(ac), jnp.cos(ac)], axis=1)
    sin32 = jnp.concatenate([jnp.sin(ar), jnp.sin(ar), jnp.sin(ac), jnp.sin(ac)], axis=1)
    ones = jnp.ones((seq, QK_NOPE), F32)
    tail = HEAD_PAD - QK_NOPE - QK_ROPE
    cos_t = jnp.concatenate([ones, cos32, jnp.zeros((seq, tail), F32)], axis=1)
    sin_t = jnp.concatenate([jnp.zeros((seq, QK_NOPE), F32), sin32, jnp.zeros((seq, tail), F32)], axis=1)
    ident = jnp.concatenate([jnp.ones((pad_rows, QK_NOPE + QK_ROPE), F32), jnp.zeros((pad_rows, tail), F32)], axis=1)
    cos_t = jnp.concatenate([cos_t, ident], axis=0)
    sin_t = jnp.concatenate([sin_t, jnp.zeros((pad_rows, HEAD_PAD), F32)], axis=0)
    return cos_t, sin_t


def _rot_half(w):
    h = AXIS_ROPE // 2
    r1, r2, c1, c2 = w[..., 0:h], w[..., h:2 * h], w[..., 2 * h:3 * h], w[..., 3 * h:4 * h]
    return jnp.concatenate([-r2, r1, -c2, c1], axis=-1)


def _layer_weights(w_in, w_uq, w_ukv):
    d = w_in.shape[0]
    kr = w_in[:, OFF_KR:OFF_KR + QK_ROPE]
    w1 = jnp.concatenate([w_in[:, :OFF_KR], jnp.zeros((d, QK_NOPE), F32), kr, _rot_half(kr)], axis=1).astype(BF16)

    uq = w_uq.reshape(Q_LORA, MLA_HEADS, QK_NOPE + QK_ROPE)
    zq = jnp.zeros((Q_LORA, MLA_HEADS, HEAD_PAD - QK_NOPE - QK_ROPE), F32)
    wq_a = jnp.concatenate([uq, zq], axis=-1).reshape(Q_LORA, QK_W)
    wq_b = _rot_half(uq[..., QK_NOPE:]).reshape(Q_LORA, MLA_HEADS * QK_ROPE)
    wq = jnp.concatenate([wq_a, wq_b], axis=1).T.astype(BF16)

    ukv = w_ukv.reshape(KV_LORA, MLA_HEADS, QK_NOPE + V_DIM)
    wk = jnp.concatenate([ukv[..., :QK_NOPE], jnp.zeros((KV_LORA, MLA_HEADS, HEAD_PAD - QK_NOPE), F32)],
                         axis=-1).reshape(KV_LORA, QK_W)
    wvt = jnp.concatenate([ukv[..., QK_NOPE:], jnp.zeros((KV_LORA, MLA_HEADS, VT_ROWS - V_DIM), F32)],
                          axis=-1).reshape(KV_LORA, MLA_HEADS * VT_ROWS).T
    return w1, wq, wk.astype(BF16), wvt.astype(BF16)


def kernel(x, c, ctx, c_ctx, w_mod, b_mod, ffn1_norm, ffn1_w_in, ffn1_w_out, mix_norm, w_in, conv_w, q_norm, w_uq,
           kv_norm, w_ukv, w_out, ffn2_norm, ffn2_w_in, ffn2_w_out, final_norm):
    batch, seq, d = x.shape
    n_ctx = ctx.shape[1]
    depth = w_mod.shape[0]
    assert d == D_MODEL and seq % GRID_W == 0 and seq % n_ctx == 0 and n_ctx % SUBLANES == 0
    assert seq & (seq - 1) == 0 and n_ctx & (n_ctx - 1) == 0
    tm = 512 if (seq % 512 == 0 and (batch * n_ctx) % 512 == 0) else n_ctx
    geo = dict(tm=tm, seq=seq, ctx=n_ctx, batch=batch)

    s = (x.reshape(batch * seq, d), ctx.reshape(batch * n_ctx, d))
    cond = jnp.concatenate([c, c_ctx[None, :]], axis=0)
    mods = _modulation(cond, w_mod, b_mod).reshape(depth, batch + 1, N_MOD, d)

    cos_t, sin_t = _rope_tables(seq, tm)
    cm_x, sm_x = _dft_mats(seq, seq // 2)
    cm_c, sm_c = _dft_mats(n_ctx)
    bdc, bds = _channel_dft_mats()
    one_col = jnp.asarray((np.arange(MLA_HEADS * VT_ROWS) % VT_ROWS >= V_DIM).astype(np.float32)[:, None])

    for l in range(depth):
        last = l == depth - 1
        m = mods[l]
        w1, wq, wk, wvt = _layer_weights(w_in[l], w_uq[l], w_ukv[l])
        s = _ffn_call(s, m, ffn1_norm[l], ffn1_w_in[l].astype(BF16), ffn1_w_out[l].astype(BF16), geo=geo, mod_idx=0)
        zc, zs, gb, u, k, qt, vt = _proj_call(s, m, mix_norm[l], w1, q_norm[l], wq, kv_norm[l], wk, wvt, one_col,
                                              cos_t, sin_t, bdc, bds, geo=geo)
        att = _attn_call(qt, k, vt, geo=geo)
        fz = _dft_folded_call(cm_x, sm_x, zc, zs, n=seq, count=batch)
        if not last:
            att = (att, _attn_ctx_call(qt, k, vt, geo=geo))
            fz = (fz, _dft_dense_call(cm_c, sm_c, zc, zs, n=n_ctx, row0=batch * seq, count=batch))
        s = _ffn_call(s, m, ffn2_norm[l], ffn2_w_in[l].astype(BF16), ffn2_w_out[l].astype(BF16), geo=geo, mod_idx=2,
                      mix=(fz, gb, u, att, conv_w[l], w_out[l].astype(BF16)),
                      final_g=final_norm if last else None, x_only=last)
    return s.reshape(batch, seq, d)
```

```python
import functools

import numpy as np
import jax
import jax.numpy as jnp
from jax import lax
from jax.experimental import pallas as pl
from jax.experimental.pallas import tpu as pltpu

D_MODEL = 1024
D_FF = 2816
N_MOD = 9
EPS = 1e-6
GRID_W = 64
F_W = 256
F_GROUPS = 4
F_GROUP_DIM = 64
CONV_W = 256
MLA_HEADS = 8
QK_NOPE = 64
QK_ROPE = 32
V_DIM = 64
Q_LORA = 384
KV_LORA = 256
MLA_OUT = MLA_HEADS * V_DIM
ROPE_BASE = 10000.0
AXIS_ROPE = QK_ROPE // 2
ATTN_SCALE = (QK_NOPE + QK_ROPE) ** -0.5
OFF_Q = 1024
OFF_KV = OFF_Q + Q_LORA
OFF_KR = OFF_KV + KV_LORA

LANES = 128
SUBLANES = 8
HEAD_PAD = LANES
QK_W = MLA_HEADS * HEAD_PAD
VT_ROWS = V_DIM + 16
LOG2E = 1.4426950408889634
ATTN_HEADS_PER_STEP = 4
SCORE_LOOKAHEAD = 1
LAG_MAX_JUMP = 64.0
FFN_ROWS = 512
FF_CHUNK = 256
VMEM_LIMIT = 56 * 1024 * 1024

BF16 = jnp.bfloat16
F32 = jnp.float32


def _dot(a, b):
    return jnp.dot(a, b, preferred_element_type=F32)


def _dot_nt(a, b):
    return lax.dot_general(a, b, (((1,), (1,)), ((), ())), preferred_element_type=F32)


def _rms(x, g):
    return x * lax.rsqrt(jnp.mean(x * x, axis=-1, keepdims=True) + EPS) * g


def _params(n_axes, flags=None):
    return pltpu.CompilerParams(dimension_semantics=("arbitrary",) * n_axes,
                                vmem_limit_bytes=VMEM_LIMIT, flags=flags)


def _const_spec(shape):
    nd = len(shape)
    return pl.BlockSpec(shape, lambda *_: (0,) * nd, pipeline_mode=pl.Buffered(1))


def _layer_spec(shape, layer):
    nd = len(shape)
    return pl.BlockSpec((None,) + tuple(shape), lambda *_: (layer,) + (0,) * nd, pipeline_mode=pl.Buffered(1))


def _mod_kernel(c_ref, w_ref, b_ref, o_ref):
    c = c_ref[...]
    a = (c * jax.nn.sigmoid(c)).astype(BF16)
    o_ref[...] = _dot(a, w_ref[...].astype(BF16)) + b_ref[...]


def _modulation(cond, w_mod, b_mod):
    n_layers, d, n = w_mod.shape
    r = cond.shape[0]
    tn = 1024
    return pl.pallas_call(
        _mod_kernel,
        out_shape=jax.ShapeDtypeStruct((n_layers, r, n), F32),
        grid=(n_layers, n // tn),
        in_specs=[
            pl.BlockSpec((r, d), lambda l, j: (0, 0)),
            pl.BlockSpec((None, d, tn), lambda l, j: (l, 0, j)),
            pl.BlockSpec((None, 1, tn), lambda l, j: (l, 0, j)),
        ],
        out_specs=pl.BlockSpec((None, r, tn), lambda l, j: (l, 0, j)),
        compiler_params=_params(2),
        name="modulation",
    )(cond, w_mod, b_mod.reshape(n_layers, 1, n))


def _swiglu_ffn(h, w_in_ref, w_out_ref):
    acc = jnp.zeros((h.shape[0], D_MODEL), F32)
    for c in range(D_FF // FF_CHUNK):
        lo = c * FF_CHUNK
        g = _dot(h, w_in_ref[:, lo:lo + FF_CHUNK])
        u = _dot(h, w_in_ref[:, D_FF + lo:D_FF + lo + FF_CHUNK])
        a = (g * jax.nn.sigmoid(g) * u).astype(BF16)
        acc = acc + _dot(a, w_out_ref[lo:lo + FF_CHUNK, :])
    return acc


def _ffn_kernel(*refs, tm, seq, ctx, n_x_tiles, mod_idx, pre_mix, final_norm, split_rows):
    it = iter(refs)
    i = pl.program_id(0)
    is_ctx = i >= n_x_tiles

    def rows(n_split):
        x_ref = next(it)
        if not n_split:
            return x_ref[...]
        c_ref = next(it)
        return jnp.where(is_ctx, c_ref[...], x_ref[...])

    s = rows(split_rows == "stream")
    m_ref, g_ref, w_in_ref, w_out_ref = (next(it) for _ in range(4))
    if pre_mix:
        fz = rows(split_rows == "mix")
        gb_ref, u_ref, up_ref, un_ref = (next(it) for _ in range(4))
        att = rows(split_rows == "mix")
        cw_ref, wo_ref = next(it), next(it)
    if final_norm:
        fg_ref = next(it)
    o_ref = next(it)

    if pre_mix:
        u = u_ref[...]
        row = lax.broadcasted_iota(jnp.int32, u.shape, 0)
        u_prev = jnp.where(row == 0, up_ref[SUBLANES - 1:SUBLANES, :], pltpu.roll(u, 1, axis=0))
        u_next = jnp.where(row == tm - 1, un_ref[0:1, :], pltpu.roll(u, tm - 1, axis=0))
        grow = row + i * tm
        len_mask = jnp.where(is_ctx, ctx - 1, seq - 1)
        pos = grow & len_mask
        u_prev = jnp.where(pos == 0, 0.0, u_prev)
        u_next = jnp.where(pos == len_mask, 0.0, u_next)
        cv = gb_ref[...] * (cw_ref[0:1, :] * u_prev + cw_ref[1:2, :] * u + cw_ref[2:3, :] * u_next)
        cat = jnp.concatenate([fz, cv.astype(BF16), att], axis=-1)
        s = s + m_ref[5:6, :] * _dot(cat, wo_ref[...])

    k = 3 * mod_idx
    sub = min(FFN_ROWS, tm)
    for lo in range(0, tm, sub):
        sr = s[lo:lo + sub, :]
        h = _rms(sr, g_ref[...]) * (1.0 + m_ref[k + 1:k + 2, :]) + m_ref[k:k + 1, :]
        y = _swiglu_ffn(h.astype(BF16), w_in_ref, w_out_ref)
        out = sr + 0.5 * m_ref[k + 2:k + 3, :] * y
        if final_norm:
            out = _rms(out, fg_ref[...])
        o_ref[lo:lo + sub, :] = out


def _ffn_call(s, m, g, w_in, w_out, *, geo, layer, mod_idx, mix=None, final_g=None, x_only=False):
    tm, seq, ctx, batch = geo["tm"], geo["seq"], geo["ctx"], geo["batch"]
    t_all = batch * (seq + ctx)
    tps = seq // tm
    n_x_tiles = batch * tps
    n_tiles = n_x_tiles if x_only else t_all // tm
    rows8 = t_all // SUBLANES

    row_spec = lambda w: pl.BlockSpec((tm, w), lambda i: (i, 0))
    in_specs, args = [], []

    def add_rows(a, w):
        if isinstance(a, tuple):
            in_specs.extend([pl.BlockSpec((tm, w), lambda i: (jnp.minimum(i, n_x_tiles - 1), 0)),
                             pl.BlockSpec((tm, w), lambda i: (jnp.maximum(i - n_x_tiles, 0), 0))])
            args.extend(a)
        else:
            in_specs.append(row_spec(w))
            args.append(a)

    add_rows(s, D_MODEL)
    in_specs += [
        pl.BlockSpec((None, N_MOD, D_MODEL), lambda i: (jnp.minimum(i // tps, batch), 0, 0)),
        _const_spec((1, D_MODEL)),
        _layer_spec((D_MODEL, 2 * D_FF), layer),
        _layer_spec((D_FF, D_MODEL), layer),
    ]
    args += [m, g.reshape(1, D_MODEL), w_in, w_out]
    split_rows = "stream" if isinstance(s, tuple) else None
    if mix is not None:
        fz, gb, u, att, conv_w, w_o = mix
        assert isinstance(fz, tuple) == isinstance(att, tuple) and split_rows is None
        split_rows = "mix" if isinstance(fz, tuple) else None
        blk = tm // SUBLANES
        add_rows(fz, F_W)
        in_specs += [
            row_spec(CONV_W), row_spec(CONV_W),
            pl.BlockSpec((SUBLANES, CONV_W), lambda i: (jnp.maximum(i * blk - 1, 0), 0)),
            pl.BlockSpec((SUBLANES, CONV_W), lambda i: (jnp.minimum((i + 1) * blk, rows8 - 1), 0)),
        ]
        args += [gb, u, u, u]
        add_rows(att, MLA_OUT)
        in_specs += [_const_spec((3, CONV_W)), _layer_spec((D_MODEL, D_MODEL), layer)]
        args += [conv_w, w_o]
    if final_g is not None:
        in_specs.append(_const_spec((1, D_MODEL)))
        args.append(final_g.reshape(1, D_MODEL))

    kern = functools.partial(_ffn_kernel, tm=tm, seq=seq, ctx=ctx, n_x_tiles=n_x_tiles, mod_idx=mod_idx,
                             pre_mix=mix is not None, final_norm=final_g is not None, split_rows=split_rows)
    return pl.pallas_call(
        kern,
        out_shape=jax.ShapeDtypeStruct((n_tiles * tm, D_MODEL), F32),
        grid=(n_tiles,),
        in_specs=in_specs,
        out_specs=row_spec(D_MODEL),
        compiler_params=_params(1),
        name="ffn_mix" if mix is not None else "ffn",
    )(*args)


def _proj_kernel(s_ref, m_ref, g_ref, w1_ref, qg_ref, wq_ref, kg_ref, wk_ref, wvt_ref, one_ref, cos_ref, sin_ref,
                 bdc_ref, bds_ref, zc_ref, zs_ref, gb_ref, u_ref, k_ref, qt_ref, vt_ref):
    s = s_ref[...]
    h = _rms(s, g_ref[...]) * (1.0 + m_ref[4:5, :]) + m_ref[3:4, :]
    p = _dot(h.astype(BF16), w1_ref[...])

    f = p[:, 0:F_W].astype(BF16)
    zc_ref[...] = _dot(f, bdc_ref[...]).astype(BF16)
    zs_ref[...] = _dot(f, bds_ref[...]).astype(BF16)
    gb_ref[...] = p[:, F_W:F_W + CONV_W]
    u_ref[...] = p[:, F_W + CONV_W:F_W + 2 * CONV_W] * p[:, F_W + 2 * CONV_W:F_W + 3 * CONV_W]

    cos = cos_ref[...]
    sin = sin_ref[...]
    qn = _rms(p[:, OFF_Q:OFF_KV], qg_ref[...]).astype(BF16)
    qq_t = _dot_nt(wq_ref[...], qn)
    kn = _rms(p[:, OFF_KV:OFF_KR], kg_ref[...]).astype(BF16)
    kk = _dot(kn, wk_ref[...])
    kp = p[:, OFF_KR:OFF_KR + HEAD_PAD]
    kr = kp * cos + pltpu.roll(kp, HEAD_PAD - QK_ROPE, axis=1) * sin
    rope = slice(QK_NOPE, QK_NOPE + QK_ROPE)
    cos_t = cos.T * (ATTN_SCALE * LOG2E)
    sin_r = sin.T[rope, :] * (ATTN_SCALE * LOG2E)
    for hd in range(MLA_HEADS):
        sl = slice(hd * HEAD_PAD, (hd + 1) * HEAD_PAD)
        a = qq_t[sl, :] * cos_t
        b = qq_t[QK_W + hd * QK_ROPE:QK_W + (hd + 1) * QK_ROPE, :] * sin_r
        q_h = jnp.concatenate([a[:QK_NOPE], a[rope] + b, a[QK_NOPE + QK_ROPE:]], axis=0)
        qt_ref[sl, :] = q_h.astype(BF16)
        k_ref[:, sl] = (kk[:, sl] + kr).astype(BF16)
    vt_ref[...] = (_dot_nt(wvt_ref[...], kn) + one_ref[...]).astype(BF16)


def _proj_call(s, m, g, w1, qg, wq, kg, wk, wvt, one_col, cos_t, sin_t, bdc, bds, *, geo):
    tm, seq, batch = geo["tm"], geo["seq"], geo["batch"]
    t_all = s.shape[0]
    tps = seq // tm
    n_x_tiles = batch * tps
    row_spec = lambda w: pl.BlockSpec((tm, w), lambda i: (i, 0))
    tab_spec = pl.BlockSpec((tm, HEAD_PAD), lambda i: (jnp.where(i < n_x_tiles, i % tps, tps), 0))
    out_widths = [(F_W, BF16), (F_W, BF16), (CONV_W, F32), (CONV_W, F32), (QK_W, BF16)]
    vt_rows = MLA_HEADS * VT_ROWS
    return pl.pallas_call(
        _proj_kernel,
        out_shape=[jax.ShapeDtypeStruct((t_all, w), dt) for w, dt in out_widths]
        + [jax.ShapeDtypeStruct((QK_W, t_all), BF16), jax.ShapeDtypeStruct((vt_rows, t_all), BF16)],
        grid=(t_all // tm,),
        in_specs=[
            row_spec(D_MODEL),
            pl.BlockSpec((None, N_MOD, D_MODEL), lambda i: (jnp.minimum(i // tps, batch), 0, 0)),
            _const_spec((1, D_MODEL)),
            _const_spec(w1.shape),
            _const_spec((1, Q_LORA)),
            _const_spec(wq.shape),
            _const_spec((1, KV_LORA)),
            _const_spec(wk.shape),
            _const_spec(wvt.shape),
            _const_spec(one_col.shape),
            tab_spec, tab_spec,
            _const_spec(bdc.shape),
            _const_spec(bds.shape),
        ],
        out_specs=[row_spec(w) for w, _ in out_widths]
        + [pl.BlockSpec((QK_W, tm), lambda i: (0, i)), pl.BlockSpec((vt_rows, tm), lambda i: (0, i))],
        compiler_params=_params(1),
        name="in_proj",
    )(s, m, g.reshape(1, D_MODEL), w1, qg.reshape(1, Q_LORA), wq, kg.reshape(1, KV_LORA), wk, wvt, one_col,
      cos_t, sin_t, bdc, bds)


def _attn_heads(qts, key_chunks, lagged):
    heads = range(len(qts))
    n_chunks = len(key_chunks[0])
    mx = [None for _ in heads]
    acc = [None for _ in heads]
    jump = None
    ahead = min(SCORE_LOOKAHEAD, n_chunks)
    scores = [[_dot(key_chunks[h][c][0], qts[h]) for c in range(ahead)] for h in heads]
    for c in range(n_chunks):
        for h in heads:
            if c + ahead < n_chunks:
                scores[h].append(_dot(key_chunks[h][c + ahead][0], qts[h]))
            s = scores[h].pop(0)
            vt = key_chunks[h][c][1]
            cmax = jnp.max(s, axis=0, keepdims=True)
            if c == 0:
                mx[h] = cmax
                acc[h] = _dot(vt, jnp.exp2(s - cmax).astype(BF16))
                continue
            new = jnp.maximum(mx[h], cmax)
            if lagged:
                acc[h] = (acc[h] + _dot(vt, jnp.exp2(s - mx[h]).astype(BF16))) * jnp.exp2(mx[h] - new)
                step = cmax - mx[h]
                jump = step if jump is None else jnp.maximum(jump, step)
            else:
                acc[h] = acc[h] * jnp.exp2(mx[h] - new) + _dot(vt, jnp.exp2(s - new).astype(BF16))
            mx[h] = new
    outs = [a[0:V_DIM, :] / a[V_DIM:V_DIM + 1, :] for a in acc]
    return outs, (None if jump is None else jnp.max(jump))


def _attn_kernel(qt_ref, kx_ref, kc_ref, vtx_ref, vtc_ref, o_ref, *, key_chunk):
    qts, chunks = [], []
    for hd in range(qt_ref.shape[0] // HEAD_PAD):
        sl = slice(hd * HEAD_PAD, (hd + 1) * HEAD_PAD)
        vsl = slice(hd * VT_ROWS, (hd + 1) * VT_ROWS)
        qts.append(qt_ref[sl, :])
        chunks.append([(kc_ref[:, sl], vtc_ref[vsl, :])]
                      + [(kx_ref[lo:lo + key_chunk, sl], vtx_ref[vsl, lo:lo + key_chunk])
                         for lo in range(0, kx_ref.shape[0], key_chunk)])
    outs, jump = _attn_heads(qts, chunks, lagged=True)
    o_ref[...] = jnp.concatenate(outs, axis=0).T.astype(BF16)

    @pl.when(jnp.logical_not(jump <= LAG_MAX_JUMP))
    def _():
        exact, _ = _attn_heads(qts, chunks, lagged=False)
        o_ref[...] = jnp.concatenate(exact, axis=0).T.astype(BF16)


def _attn_ctx_kernel(qt_ref, kc_ref, vtc_ref, o_ref):
    qts, chunks = [], []
    for hd in range(qt_ref.shape[0] // HEAD_PAD):
        sl = slice(hd * HEAD_PAD, (hd + 1) * HEAD_PAD)
        qts.append(qt_ref[sl, :])
        chunks.append([(kc_ref[:, sl], vtc_ref[hd * VT_ROWS:(hd + 1) * VT_ROWS, :])])
    outs, _ = _attn_heads(qts, chunks, lagged=False)
    o_ref[...] = jnp.concatenate(outs, axis=0).T.astype(BF16)


def _attn_call(qt, k, vt, *, geo):
    seq, ctx, batch = geo["seq"], geo["ctx"], geo["batch"]
    tq = min(512, seq)
    nq = seq // tq
    ctx_blk0 = batch * seq // ctx
    hps = ATTN_HEADS_PER_STEP
    return pl.pallas_call(
        functools.partial(_attn_kernel, key_chunk=min(256, seq)),
        out_shape=jax.ShapeDtypeStruct((batch * seq, MLA_OUT), BF16),
        grid=(batch, MLA_HEADS // hps, nq),
        in_specs=[
            pl.BlockSpec((hps * HEAD_PAD, tq), lambda b, p, qi: (p, b * nq + qi)),
            pl.BlockSpec((seq, hps * HEAD_PAD), lambda b, p, qi: (b, p)),
            pl.BlockSpec((ctx, hps * HEAD_PAD), lambda b, p, qi: (ctx_blk0 + b, p)),
            pl.BlockSpec((hps * VT_ROWS, seq), lambda b, p, qi: (p, b)),
            pl.BlockSpec((hps * VT_ROWS, ctx), lambda b, p, qi: (p, ctx_blk0 + b)),
        ],
        out_specs=pl.BlockSpec((tq, hps * V_DIM), lambda b, p, qi: (b * nq + qi, p)),
        compiler_params=_params(3),
        name="attention",
    )(qt, k, k, vt, vt)


def _attn_ctx_call(qt, k, vt, *, geo):
    seq, ctx, batch = geo["seq"], geo["ctx"], geo["batch"]
    ctx_blk0 = batch * seq // ctx
    return pl.pallas_call(
        _attn_ctx_kernel,
        out_shape=jax.ShapeDtypeStruct((batch * ctx, MLA_OUT), BF16),
        grid=(batch,),
        in_specs=[
            pl.BlockSpec((QK_W, ctx), lambda b: (0, ctx_blk0 + b)),
            pl.BlockSpec((ctx, QK_W), lambda b: (ctx_blk0 + b, 0)),
            pl.BlockSpec((MLA_HEADS * VT_ROWS, ctx), lambda b: (0, ctx_blk0 + b)),
        ],
        out_specs=pl.BlockSpec((ctx, MLA_OUT), lambda b: (b, 0)),
        compiler_params=_params(1),
        name="attention_context",
    )(qt, k, vt)


def _dft_kernel(c_ref, s_ref, zc_ref, zs_ref, o_ref):
    o_ref[...] = (_dot(c_ref[...], zc_ref[...]) - _dot(s_ref[...], zs_ref[...])).astype(BF16)


def _dft_dense_call(cm, sm, zc, zs, *, n, row0, count):
    tp = min(512, n)
    npt = n // tp
    blk0 = row0 // n
    return pl.pallas_call(
        _dft_kernel,
        out_shape=jax.ShapeDtypeStruct((count * n, F_W), BF16),
        grid=(npt, count),
        in_specs=[
            pl.BlockSpec((tp, n), lambda p, b: (p, 0)),
            pl.BlockSpec((tp, n), lambda p, b: (p, 0)),
            pl.BlockSpec((n, F_W), lambda p, b: (blk0 + b, 0)),
            pl.BlockSpec((n, F_W), lambda p, b: (blk0 + b, 0)),
        ],
        out_specs=pl.BlockSpec((tp, F_W), lambda p, b: (b * npt + p, 0)),
        compiler_params=_params(2),
        name="dft_dense",
    )(cm, sm, zc, zs)


def _flip_rows(flip_ref, src_ref, lo, hi, ft):
    return jnp.concatenate([_dot(flip_ref[...], src_ref[hi - (t + 1) * ft:hi - t * ft, :])
                            for t in range((hi - lo) // ft)], axis=0)


def _dft_folded_kernel(c_ref, s_ref, zc_ref, zs_ref, flip_ref, o_ref, ec_ref, os_ref, g_ref, mid_ref, *, tp, ft):
    n = zc_ref.shape[0]
    half = n // 2
    p = pl.program_id(1)
    scale = n ** -0.5

    @pl.when(p == 0)
    def _():
        row = lax.broadcasted_iota(jnp.int32, (half, F_W), 0)
        for z_ref, dst_ref, sign in ((zc_ref, ec_ref, 1.0), (zs_ref, os_ref, -1.0)):
            rev = _flip_rows(flip_ref, z_ref, half, n, ft)
            mirror = jnp.where(row == 0, 0.0, pltpu.roll(rev, 1, axis=0))
            folded = z_ref[0:half, :].astype(F32) + sign * mirror
            dst_ref[...] = folded.astype(BF16)
            if z_ref is zc_ref:
                alt_sum = jnp.sum(jnp.where((row & 1) == 0, folded, -folded), axis=0, keepdims=True)
                mid_ref[0:1, :] = (alt_sum + zc_ref[half:half + 1, :].astype(F32)) * scale

    rows = pl.ds(pl.multiple_of(p * tp, tp), tp)
    e = _dot(c_ref[rows, :], ec_ref[...])
    o = _dot(s_ref[rows, :], os_ref[...])
    k = p * tp + lax.broadcasted_iota(jnp.int32, (tp, F_W), 0)
    e = e + jnp.where((k & 1) == 0, scale, -scale) * zc_ref[half:half + 1, :].astype(F32)
    o_ref[rows, :] = (e - o).astype(BF16)
    g_ref[rows, :] = (e + o).astype(BF16)

    @pl.when(p == pl.num_programs(1) - 1)
    def _():
        row = lax.broadcasted_iota(jnp.int32, (half, F_W), 0)
        rev = _flip_rows(flip_ref, g_ref, 0, half, ft)
        upper = jnp.where(row == 0, mid_ref[0:1, :], pltpu.roll(rev, 1, axis=0))
        o_ref[half:n, :] = upper.astype(BF16)


def _dft_folded_call(cm, sm, zc, zs, *, n, count):
    half = n // 2
    tp = min(512, half)
    ft = min(512, half)
    flip = jnp.asarray(np.eye(ft)[::-1], BF16)
    seq_spec = pl.BlockSpec((n, F_W), lambda b, p: (b, 0))
    return pl.pallas_call(
        functools.partial(_dft_folded_kernel, tp=tp, ft=ft),
        out_shape=jax.ShapeDtypeStruct((count * n, F_W), BF16),
        grid=(count, half // tp),
        in_specs=[_const_spec((half, half)), _const_spec((half, half)), seq_spec, seq_spec, _const_spec((ft, ft))],
        out_specs=seq_spec,
        scratch_shapes=[pltpu.VMEM((half, F_W), BF16), pltpu.VMEM((half, F_W), BF16), pltpu.VMEM((half, F_W), BF16),
                        pltpu.VMEM((SUBLANES, F_W), F32)],
        compiler_params=_params(2),
        name="dft_folded",
    )(cm, sm, zc, zs, flip)


def _dft_mats(n, size=None):
    rows = cols = n if size is None else size
    k = jnp.arange(rows, dtype=jnp.int32)[:, None]
    nb = min(LANES, cols)
    na = cols // nb
    a = jnp.arange(na, dtype=jnp.int32)[None, :]
    b = jnp.arange(nb, dtype=jnp.int32)[None, :]
    w = 2.0 * np.pi / n
    ang1 = ((k * a * nb) % n).astype(F32) * w
    ang2 = ((k * b) % n).astype(F32) * w
    c1, s1, c2, s2 = jnp.cos(ang1), jnp.sin(ang1), jnp.cos(ang2), jnp.sin(ang2)
    scale = n ** -0.5
    cm = (c1[:, :, None] * c2[:, None, :] - s1[:, :, None] * s2[:, None, :]).reshape(rows, cols) * scale
    sm = (s1[:, :, None] * c2[:, None, :] + c1[:, :, None] * s2[:, None, :]).reshape(rows, cols) * scale
    return cm.astype(BF16), sm.astype(BF16)


def _channel_dft_mats():
    j = np.arange(F_GROUP_DIM)
    ang = 2.0 * np.pi * ((j[:, None] * j[None, :]) % F_GROUP_DIM) / F_GROUP_DIM
    eye = np.eye(F_GROUPS)
    scale = F_GROUP_DIM ** -0.5
    bdc = np.kron(eye, np.cos(ang) * scale)
    bds = np.kron(eye, np.sin(ang) * scale)
    return jnp.asarray(bdc, BF16), jnp.asarray(bds, BF16)


def _rope_tables(seq, pad_rows):
    rows = seq // GRID_W
    row = jnp.repeat(jnp.arange(rows), GRID_W).astype(F32)
    col = jnp.tile(jnp.arange(GRID_W), rows).astype(F32)
    inv = 1.0 / (ROPE_BASE ** (jnp.arange(0, AXIS_ROPE, 2, dtype=F32) / AXIS_ROPE))
    ar = row[:, None] * inv
    ac = col[:, None] * inv
    cos32 = jnp.concatenate([jnp.cos(ar), jnp.cos(ar), jnp.cos(ac), jnp.cos(ac)], axis=1)
    sin32 = jnp.concatenate([jnp.sin(ar), jnp.sin(ar), jnp.sin(ac), jnp.sin(ac)], axis=1)
    ones = jnp.ones((seq, QK_NOPE), F32)
    tail = HEAD_PAD - QK_NOPE - QK_ROPE
    cos_t = jnp.concatenate([ones, cos32, jnp.zeros((seq, tail), F32)], axis=1)
    sin_t = jnp.concatenate([jnp.zeros((seq, QK_NOPE), F32), sin32, jnp.zeros((seq, tail), F32)], axis=1)
    ident = jnp.concatenate([jnp.ones((pad_rows, QK_NOPE + QK_ROPE), F32), jnp.zeros((pad_rows, tail), F32)], axis=1)
    cos_t = jnp.concatenate([cos_t, ident], axis=0)
    sin_t = jnp.concatenate([sin_t, jnp.zeros((pad_rows, HEAD_PAD), F32)], axis=0)
    return cos_t, sin_t


def _rot_half(w):
    h = AXIS_ROPE // 2
    r1, r2, c1, c2 = w[..., 0:h], w[..., h:2 * h], w[..., 2 * h:3 * h], w[..., 3 * h:4 * h]
    return jnp.concatenate([-r2, r1, -c2, c1], axis=-1)


def _layer_weights(w_in, w_uq, w_ukv):
    d = w_in.shape[0]
    kr = w_in[:, OFF_KR:OFF_KR + QK_ROPE]
    w1 = jnp.concatenate([w_in[:, :OFF_KR], jnp.zeros((d, QK_NOPE), F32), kr, _rot_half(kr)], axis=1).astype(BF16)

    uq = w_uq.reshape(Q_LORA, MLA_HEADS, QK_NOPE + QK_ROPE)
    zq = jnp.zeros((Q_LORA, MLA_HEADS, HEAD_PAD - QK_NOPE - QK_ROPE), F32)
    wq_a = jnp.concatenate([uq, zq], axis=-1).reshape(Q_LORA, QK_W)
    wq_b = _rot_half(uq[..., QK_NOPE:]).reshape(Q_LORA, MLA_HEADS * QK_ROPE)
    wq = jnp.concatenate([wq_a, wq_b], axis=1).T.astype(BF16)

    ukv = w_ukv.reshape(KV_LORA, MLA_HEADS, QK_NOPE + V_DIM)
    wk = jnp.concatenate([ukv[..., :QK_NOPE], jnp.zeros((KV_LORA, MLA_HEADS, HEAD_PAD - QK_NOPE), F32)],
                         axis=-1).reshape(KV_LORA, QK_W)
    wvt = jnp.concatenate([ukv[..., QK_NOPE:], jnp.zeros((KV_LORA, MLA_HEADS, VT_ROWS - V_DIM), F32)],
                          axis=-1).reshape(KV_LORA, MLA_HEADS * VT_ROWS).T
    return w1, wq, wk.astype(BF16), wvt.astype(BF16)


def kernel(x, c, ctx, c_ctx, w_mod, b_mod, ffn1_norm, ffn1_w_in, ffn1_w_out, mix_norm, w_in, conv_w, q_norm, w_uq,
           kv_norm, w_ukv, w_out, ffn2_norm, ffn2_w_in, ffn2_w_out, final_norm):
    batch, seq, d = x.shape
    n_ctx = ctx.shape[1]
    depth = w_mod.shape[0]
    assert d == D_MODEL and seq % GRID_W == 0 and seq % n_ctx == 0 and n_ctx % SUBLANES == 0
    assert seq & (seq - 1) == 0 and n_ctx & (n_ctx - 1) == 0
    tm = 1024 if (seq % 1024 == 0 and (batch * n_ctx) % 1024 == 0) else n_ctx
    geo = dict(tm=tm, seq=seq, ctx=n_ctx, batch=batch)

    s = (x.reshape(batch * seq, d), ctx.reshape(batch * n_ctx, d))
    cond = jnp.concatenate([c, c_ctx[None, :]], axis=0)
    mods = _modulation(cond, w_mod, b_mod).reshape(depth, batch + 1, N_MOD, d)

    cos_t, sin_t = _rope_tables(seq, tm)
    cm_x, sm_x = _dft_mats(seq, seq // 2)
    cm_c, sm_c = _dft_mats(n_ctx)
    bdc, bds = _channel_dft_mats()
    one_col = jnp.asarray((np.arange(MLA_HEADS * VT_ROWS) % VT_ROWS >= V_DIM).astype(np.float32)[:, None])
    ffn1_w = (ffn1_w_in.astype(BF16), ffn1_w_out.astype(BF16))
    ffn2_w = (ffn2_w_in.astype(BF16), ffn2_w_out.astype(BF16))
    w_o = w_out.astype(BF16)

    for l in range(depth):
        last = l == depth - 1
        m = mods[l]
        w1, wq, wk, wvt = _layer_weights(w_in[l], w_uq[l], w_ukv[l])
        s = _ffn_call(s, m, ffn1_norm[l], *ffn1_w, geo=geo, layer=l, mod_idx=0)
        zc, zs, gb, u, k, qt, vt = _proj_call(s, m, mix_norm[l], w1, q_norm[l], wq, kv_norm[l], wk, wvt, one_col,
                                              cos_t, sin_t, bdc, bds, geo=geo)
        att = _attn_call(qt, k, vt, geo=geo)
        fz = _dft_folded_call(cm_x, sm_x, zc, zs, n=seq, count=batch)
        if not last:
            att = (att, _attn_ctx_call(qt, k, vt, geo=geo))
            fz = (fz, _dft_dense_call(cm_c, sm_c, zc, zs, n=n_ctx, row0=batch * seq, count=batch))
        s = _ffn_call(s, m, ffn2_norm[l], *ffn2_w, geo=geo, layer=l, mod_idx=2,
                      mix=(fz, gb, u, att, conv_w[l], w_o),
                      final_g=final_norm if last else None, x_only=last)
    return s.reshape(batch, seq, d)
```

```python
import functools

import numpy as np
import jax
import jax.numpy as jnp
from jax import lax
from jax.experimental import pallas as pl
from jax.experimental.pallas import tpu as pltpu

D_MODEL = 1024
D_FF = 2816
N_MOD = 9
EPS = 1e-6
GRID_W = 64
F_W = 256
F_GROUPS = 4
F_GROUP_DIM = 64
CONV_W = 256
MLA_HEADS = 8
QK_NOPE = 64
QK_ROPE = 32
V_DIM = 64
Q_LORA = 384
KV_LORA = 256
MLA_OUT = MLA_HEADS * V_DIM
ROPE_BASE = 10000.0
AXIS_ROPE = QK_ROPE // 2
ATTN_SCALE = (QK_NOPE + QK_ROPE) ** -0.5
OFF_Q = 1024
OFF_KV = OFF_Q + Q_LORA
OFF_KR = OFF_KV + KV_LORA

LANES = 128
SUBLANES = 8
HEAD_PAD = LANES
QK_W = MLA_HEADS * HEAD_PAD
VT_ROWS = V_DIM + 16
LOG2E = 1.4426950408889634
ATTN_HEADS_PER_STEP = 4
SCORE_LOOKAHEAD = 1
LAG_MAX_JUMP = 64.0
FFN_ROWS = 512
FF_CHUNK = 256
VMEM_LIMIT = 56 * 1024 * 1024

BF16 = jnp.bfloat16
F32 = jnp.float32


def _dot(a, b):
    return jnp.dot(a, b, preferred_element_type=F32)


def _dot_nt(a, b):
    return lax.dot_general(a, b, (((1,), (1,)), ((), ())), preferred_element_type=F32)


def _rms(x, g):
    return x * lax.rsqrt(jnp.mean(x * x, axis=-1, keepdims=True) + EPS) * g


def _params(n_axes):
    return pltpu.CompilerParams(dimension_semantics=("arbitrary",) * n_axes, vmem_limit_bytes=VMEM_LIMIT)


def _const_spec(shape):
    nd = len(shape)
    return pl.BlockSpec(shape, lambda *_: (0,) * nd, pipeline_mode=pl.Buffered(1))


def _layer_spec(shape, layer):
    nd = len(shape)
    return pl.BlockSpec((None,) + tuple(shape), lambda *_: (layer,) + (0,) * nd, pipeline_mode=pl.Buffered(1))


def _mod_kernel(c_ref, w_ref, b_ref, o_ref):
    c = c_ref[...]
    a = (c * jax.nn.sigmoid(c)).astype(BF16)
    o_ref[...] = _dot(a, w_ref[...].astype(BF16)) + b_ref[...]


def _modulation(cond, w_mod, b_mod):
    n_layers, d, n = w_mod.shape
    r = cond.shape[0]
    tn = 1024
    return pl.pallas_call(
        _mod_kernel,
        out_shape=jax.ShapeDtypeStruct((n_layers, r, n), F32),
        grid=(n_layers, n // tn),
        in_specs=[
            pl.BlockSpec((r, d), lambda l, j: (0, 0)),
            pl.BlockSpec((None, d, tn), lambda l, j: (l, 0, j)),
            pl.BlockSpec((None, 1, tn), lambda l, j: (l, 0, j)),
        ],
        out_specs=pl.BlockSpec((None, r, tn), lambda l, j: (l, 0, j)),
        compiler_params=_params(2),
        name="modulation",
    )(cond, w_mod, b_mod.reshape(n_layers, 1, n))


def _swiglu_ffn(h, w_in_ref, w_out_ref):
    acc = jnp.zeros((h.shape[0], D_MODEL), F32)
    for c in range(D_FF // FF_CHUNK):
        lo = c * FF_CHUNK
        g = _dot(h, w_in_ref[:, lo:lo + FF_CHUNK])
        u = _dot(h, w_in_ref[:, D_FF + lo:D_FF + lo + FF_CHUNK])
        a = (g * jax.nn.sigmoid(g) * u).astype(BF16)
        acc = acc + _dot(a, w_out_ref[lo:lo + FF_CHUNK, :])
    return acc


def _ffn_kernel(*refs, tm, seq, ctx, n_x_tiles, mod_idx, pre_mix, final_norm, split_rows):
    it = iter(refs)
    i = pl.program_id(0)
    is_ctx = i >= n_x_tiles

    def rows(n_split):
        x_ref = next(it)
        if not n_split:
            return x_ref[...]
        c_ref = next(it)
        return jnp.where(is_ctx, c_ref[...], x_ref[...])

    s = rows(split_rows == "stream")
    m_ref, g_ref, w_in_ref, w_out_ref = (next(it) for _ in range(4))
    if pre_mix:
        fz = rows(split_rows == "mix")
        gb_ref, u_ref, up_ref, un_ref = (next(it) for _ in range(4))
        att = rows(split_rows == "mix")
        cw_ref, wo_ref = next(it), next(it)
    if final_norm:
        fg_ref = next(it)
    o_ref = next(it)

    if pre_mix:
        u = u_ref[...]
        row = lax.broadcasted_iota(jnp.int32, u.shape, 0)
        u_prev = jnp.where(row == 0, up_ref[SUBLANES - 1:SUBLANES, :], pltpu.roll(u, 1, axis=0))
        u_next = jnp.where(row == tm - 1, un_ref[0:1, :], pltpu.roll(u, tm - 1, axis=0))
        grow = row + i * tm
        len_mask = jnp.where(is_ctx, ctx - 1, seq - 1)
        pos = grow & len_mask
        u_prev = jnp.where(pos == 0, 0.0, u_prev)
        u_next = jnp.where(pos == len_mask, 0.0, u_next)
        cv = gb_ref[...] * (cw_ref[0:1, :] * u_prev + cw_ref[1:2, :] * u + cw_ref[2:3, :] * u_next)
        cat = jnp.concatenate([fz, cv.astype(BF16), att], axis=-1)
        s = s + m_ref[5:6, :] * _dot(cat, wo_ref[...])

    k = 3 * mod_idx
    sub = min(FFN_ROWS, tm)
    for lo in range(0, tm, sub):
        sr = s[lo:lo + sub, :]
        h = _rms(sr, g_ref[...]) * (1.0 + m_ref[k + 1:k + 2, :]) + m_ref[k:k + 1, :]
        y = _swiglu_ffn(h.astype(BF16), w_in_ref, w_out_ref)
        out = sr + 0.5 * m_ref[k + 2:k + 3, :] * y
        if final_norm:
            out = _rms(out, fg_ref[...])
        o_ref[lo:lo + sub, :] = out


def _ffn_call(s, m, g, w_in, w_out, *, geo, layer, mod_idx, mix=None, final_g=None, x_only=False):
    tm, seq, ctx, batch = geo["tm"], geo["seq"], geo["ctx"], geo["batch"]
    t_all = batch * (seq + ctx)
    tps = seq // tm
    n_x_tiles = batch * tps
    n_tiles = n_x_tiles if x_only else t_all // tm
    rows8 = t_all // SUBLANES

    row_spec = lambda w: pl.BlockSpec((tm, w), lambda i: (i, 0))
    in_specs, args = [], []

    def add_rows(a, w):
        if isinstance(a, tuple):
            in_specs.extend([pl.BlockSpec((tm, w), lambda i: (jnp.minimum(i, n_x_tiles - 1), 0)),
                             pl.BlockSpec((tm, w), lambda i: (jnp.maximum(i - n_x_tiles, 0), 0))])
            args.extend(a)
        else:
            in_specs.append(row_spec(w))
            args.append(a)

    add_rows(s, D_MODEL)
    in_specs += [
        pl.BlockSpec((None, N_MOD, D_MODEL), lambda i: (jnp.minimum(i // tps, batch), 0, 0)),
        _const_spec((1, D_MODEL)),
        _layer_spec((D_MODEL, 2 * D_FF), layer),
        _layer_spec((D_FF, D_MODEL), layer),
    ]
    args += [m, g.reshape(1, D_MODEL), w_in, w_out]
    split_rows = "stream" if isinstance(s, tuple) else None
    if mix is not None:
        fz, gb, u, att, conv_w, w_o = mix
        assert isinstance(fz, tuple) == isinstance(att, tuple) and split_rows is None
        split_rows = "mix" if isinstance(fz, tuple) else None
        blk = tm // SUBLANES
        add_rows(fz, F_W)
        in_specs += [
            row_spec(CONV_W), row_spec(CONV_W),
            pl.BlockSpec((SUBLANES, CONV_W), lambda i: (jnp.maximum(i * blk - 1, 0), 0)),
            pl.BlockSpec((SUBLANES, CONV_W), lambda i: (jnp.minimum((i + 1) * blk, rows8 - 1), 0)),
        ]
        args += [gb, u, u, u]
        add_rows(att, MLA_OUT)
        in_specs += [_const_spec((3, CONV_W)), _layer_spec((D_MODEL, D_MODEL), layer)]
        args += [conv_w, w_o]
    if final_g is not None:
        in_specs.append(_const_spec((1, D_MODEL)))
        args.append(final_g.reshape(1, D_MODEL))

    kern = functools.partial(_ffn_kernel, tm=tm, seq=seq, ctx=ctx, n_x_tiles=n_x_tiles, mod_idx=mod_idx,
                             pre_mix=mix is not None, final_norm=final_g is not None, split_rows=split_rows)
    return pl.pallas_call(
        kern,
        out_shape=jax.ShapeDtypeStruct((n_tiles * tm, D_MODEL), F32),
        grid=(n_tiles,),
        in_specs=in_specs,
        out_specs=row_spec(D_MODEL),
        compiler_params=_params(1),
        name="ffn_mix" if mix is not None else "ffn",
    )(*args)


def _proj_kernel(s_ref, m_ref, g_ref, w1_ref, qg_ref, wq_ref, kg_ref, wk_ref, wvt_ref, cos_ref, sin_ref,
                 bdc_ref, bds_ref, zc_ref, zs_ref, gb_ref, u_ref, k_ref, qt_ref, vt_ref):
    s = s_ref[...]
    h = _rms(s, g_ref[...]) * (1.0 + m_ref[4:5, :]) + m_ref[3:4, :]
    p = _dot(h.astype(BF16), w1_ref[...])

    f = p[:, 0:F_W].astype(BF16)
    zc_ref[...] = _dot(f, bdc_ref[...]).astype(BF16)
    zs_ref[...] = _dot(f, bds_ref[...]).astype(BF16)
    gb_ref[...] = p[:, F_W:F_W + CONV_W]
    u_ref[...] = p[:, F_W + CONV_W:F_W + 2 * CONV_W] * p[:, F_W + 2 * CONV_W:F_W + 3 * CONV_W]

    cos = cos_ref[...]
    sin = sin_ref[...]
    qn = _rms(p[:, OFF_Q:OFF_KV], qg_ref[...]).astype(BF16)
    qq_t = _dot_nt(wq_ref[...], qn)
    kn = _rms(p[:, OFF_KV:OFF_KR], kg_ref[...]).astype(BF16)
    kk = _dot(kn, wk_ref[...])
    kp = p[:, OFF_KR:OFF_KR + HEAD_PAD]
    kr = kp * cos + pltpu.roll(kp, HEAD_PAD - QK_ROPE, axis=1) * sin
    rope = slice(QK_NOPE, QK_NOPE + QK_ROPE)
    cos_t = cos.T * (ATTN_SCALE * LOG2E)
    sin_r = sin.T[rope, :] * (ATTN_SCALE * LOG2E)
    nope_lanes = lax.broadcasted_iota(jnp.int32, kr.shape, 1) < QK_NOPE
    for hd in range(MLA_HEADS):
        sl = slice(hd * HEAD_PAD, (hd + 1) * HEAD_PAD)
        a = qq_t[sl, :] * cos_t
        b = qq_t[QK_W + hd * QK_ROPE:QK_W + (hd + 1) * QK_ROPE, :] * sin_r
        q_h = jnp.concatenate([a[:QK_NOPE], a[rope] + b, a[QK_NOPE + QK_ROPE:]], axis=0)
        qt_ref[sl, :] = q_h.astype(BF16)
        pair = kk[:, (hd // 2) * HEAD_PAD:(hd // 2 + 1) * HEAD_PAD]
        nope = pair if hd % 2 == 0 else pltpu.roll(pair, QK_NOPE, axis=1)
        k_ref[:, sl] = jnp.where(nope_lanes, nope, kr).astype(BF16)
    v_t = _dot_nt(wvt_ref[...], kn).astype(BF16)
    ones = jnp.ones((VT_ROWS - V_DIM, v_t.shape[1]), BF16)
    for hd in range(MLA_HEADS):
        vt_ref[hd * VT_ROWS:hd * VT_ROWS + V_DIM, :] = v_t[hd * V_DIM:(hd + 1) * V_DIM, :]
        vt_ref[hd * VT_ROWS + V_DIM:(hd + 1) * VT_ROWS, :] = ones


def _proj_call(s, m, g, w1, qg, wq, kg, wk, wvt, cos_t, sin_t, bdc, bds, *, geo):
    tm, seq, batch = geo["tm"], geo["seq"], geo["batch"]
    t_all = s.shape[0]
    tps = seq // tm
    n_x_tiles = batch * tps
    row_spec = lambda w: pl.BlockSpec((tm, w), lambda i: (i, 0))
    tab_spec = pl.BlockSpec((tm, HEAD_PAD), lambda i: (jnp.where(i < n_x_tiles, i % tps, tps), 0))
    out_widths = [(F_W, BF16), (F_W, BF16), (CONV_W, F32), (CONV_W, F32), (QK_W, BF16)]
    vt_rows = MLA_HEADS * VT_ROWS
    return pl.pallas_call(
        _proj_kernel,
        out_shape=[jax.ShapeDtypeStruct((t_all, w), dt) for w, dt in out_widths]
        + [jax.ShapeDtypeStruct((QK_W, t_all), BF16), jax.ShapeDtypeStruct((vt_rows, t_all), BF16)],
        grid=(t_all // tm,),
        in_specs=[
            row_spec(D_MODEL),
            pl.BlockSpec((None, N_MOD, D_MODEL), lambda i: (jnp.minimum(i // tps, batch), 0, 0)),
            _const_spec((1, D_MODEL)),
            _const_spec(w1.shape),
            _const_spec((1, Q_LORA)),
            _const_spec(wq.shape),
            _const_spec((1, KV_LORA)),
            _const_spec(wk.shape),
            _const_spec(wvt.shape),
            tab_spec, tab_spec,
            _const_spec(bdc.shape),
            _const_spec(bds.shape),
        ],
        out_specs=[row_spec(w) for w, _ in out_widths]
        + [pl.BlockSpec((QK_W, tm), lambda i: (0, i)), pl.BlockSpec((vt_rows, tm), lambda i: (0, i))],
        compiler_params=_params(1),
        name="in_proj",
    )(s, m, g.reshape(1, D_MODEL), w1, qg.reshape(1, Q_LORA), wq, kg.reshape(1, KV_LORA), wk, wvt,
      cos_t, sin_t, bdc, bds)


def _attn_heads(qts, key_chunks):
    heads = range(len(qts))
    n_chunks = len(key_chunks[0])
    mx = [None for _ in heads]
    acc = [None for _ in heads]
    ahead = min(SCORE_LOOKAHEAD, n_chunks)
    scores = [[_dot(key_chunks[h][c][0], qts[h]) for c in range(ahead)] for h in heads]
    for c in range(n_chunks):
        for h in heads:
            if c + ahead < n_chunks:
                scores[h].append(_dot(key_chunks[h][c + ahead][0], qts[h]))
            s = scores[h].pop(0)
            vt = key_chunks[h][c][1]
            cmax = jnp.max(s, axis=0, keepdims=True)
            if c == 0:
                mx[h] = cmax
                acc[h] = _dot(vt, jnp.exp2(s - cmax).astype(BF16))
                continue
            new = jnp.maximum(mx[h], cmax)
            acc[h] = acc[h] * jnp.exp2(mx[h] - new) + _dot(vt, jnp.exp2(s - new).astype(BF16))
            mx[h] = new
    return [a[0:V_DIM, :] / a[V_DIM:V_DIM + 1, :] for a in acc]


def _attn_heads_lagged(qts, key_chunks):
    heads = range(len(qts))
    n_chunks = len(key_chunks[0])
    m_ref, m_run, p, acc = ([None for _ in heads] for _ in range(4))
    jump = None
    for h in heads:
        s = _dot(key_chunks[h][0][0], qts[h])
        m_ref[h] = m_run[h] = jnp.max(s, axis=0, keepdims=True)
        p[h] = jnp.exp2(s - m_run[h]).astype(BF16)
    for c in range(n_chunks):
        for h in heads:
            if c + 1 < n_chunks:
                s = _dot(key_chunks[h][c + 1][0], qts[h])
                cmax = jnp.max(s, axis=0, keepdims=True)
                step = cmax - m_run[h]
                jump = step if jump is None else jnp.maximum(jump, step)
                p_next = jnp.exp2(s - m_run[h]).astype(BF16)
            part = _dot(key_chunks[h][c][1], p[h])
            acc[h] = part if acc[h] is None else acc[h] + part
            if c + 1 < n_chunks:
                acc[h] = acc[h] * jnp.exp2(m_ref[h] - m_run[h])
                p[h] = p_next
                m_ref[h], m_run[h] = m_run[h], jnp.maximum(m_run[h], cmax)
    outs = [a[0:V_DIM, :] / a[V_DIM:V_DIM + 1, :] for a in acc]
    return outs, jnp.max(jump)


def _attn_kernel(qt_ref, kx_ref, kc_ref, vtx_ref, vtc_ref, o_ref, *, key_chunk):
    qts, chunks = [], []
    for hd in range(qt_ref.shape[0] // HEAD_PAD):
        sl = slice(hd * HEAD_PAD, (hd + 1) * HEAD_PAD)
        vsl = slice(hd * VT_ROWS, (hd + 1) * VT_ROWS)
        qts.append(qt_ref[sl, :])
        chunks.append([(kc_ref[:, sl], vtc_ref[vsl, :])]
                      + [(kx_ref[lo:lo + key_chunk, sl], vtx_ref[vsl, lo:lo + key_chunk])
                         for lo in range(0, kx_ref.shape[0], key_chunk)])
    outs, jump = _attn_heads_lagged(qts, chunks)
    o_ref[...] = jnp.concatenate(outs, axis=0).T.astype(BF16)

    @pl.when(jnp.logical_not(jump <= LAG_MAX_JUMP))
    def _():
        o_ref[...] = jnp.concatenate(_attn_heads(qts, chunks), axis=0).T.astype(BF16)


def _attn_ctx_kernel(qt_ref, kc_ref, vtc_ref, o_ref):
    qts, chunks = [], []
    for hd in range(qt_ref.shape[0] // HEAD_PAD):
        sl = slice(hd * HEAD_PAD, (hd + 1) * HEAD_PAD)
        qts.append(qt_ref[sl, :])
        chunks.append([(kc_ref[:, sl], vtc_ref[hd * VT_ROWS:(hd + 1) * VT_ROWS, :])])
    o_ref[...] = jnp.concatenate(_attn_heads(qts, chunks), axis=0).T.astype(BF16)


def _attn_call(qt, k, vt, *, geo):
    seq, ctx, batch = geo["seq"], geo["ctx"], geo["batch"]
    tq = min(512, seq)
    nq = seq // tq
    ctx_blk0 = batch * seq // ctx
    hps = ATTN_HEADS_PER_STEP
    return pl.pallas_call(
        functools.partial(_attn_kernel, key_chunk=min(256, seq)),
        out_shape=jax.ShapeDtypeStruct((batch * seq, MLA_OUT), BF16),
        grid=(batch, MLA_HEADS // hps, nq),
        in_specs=[
            pl.BlockSpec((hps * HEAD_PAD, tq), lambda b, p, qi: (p, b * nq + qi)),
            pl.BlockSpec((seq, hps * HEAD_PAD), lambda b, p, qi: (b, p)),
            pl.BlockSpec((ctx, hps * HEAD_PAD), lambda b, p, qi: (ctx_blk0 + b, p)),
            pl.BlockSpec((hps * VT_ROWS, seq), lambda b, p, qi: (p, b)),
            pl.BlockSpec((hps * VT_ROWS, ctx), lambda b, p, qi: (p, ctx_blk0 + b)),
        ],
        out_specs=pl.BlockSpec((tq, hps * V_DIM), lambda b, p, qi: (b * nq + qi, p)),
        compiler_params=_params(3),
        name="attention",
    )(qt, k, k, vt, vt)


def _attn_ctx_call(qt, k, vt, *, geo):
    seq, ctx, batch = geo["seq"], geo["ctx"], geo["batch"]
    ctx_blk0 = batch * seq // ctx
    return pl.pallas_call(
        _attn_ctx_kernel,
        out_shape=jax.ShapeDtypeStruct((batch * ctx, MLA_OUT), BF16),
        grid=(batch,),
        in_specs=[
            pl.BlockSpec((QK_W, ctx), lambda b: (0, ctx_blk0 + b)),
            pl.BlockSpec((ctx, QK_W), lambda b: (ctx_blk0 + b, 0)),
            pl.BlockSpec((MLA_HEADS * VT_ROWS, ctx), lambda b: (0, ctx_blk0 + b)),
        ],
        out_specs=pl.BlockSpec((ctx, MLA_OUT), lambda b: (b, 0)),
        compiler_params=_params(1),
        name="attention_context",
    )(qt, k, vt)


def _dft_kernel(c_ref, s_ref, zc_ref, zs_ref, o_ref):
    o_ref[...] = (_dot(c_ref[...], zc_ref[...]) - _dot(s_ref[...], zs_ref[...])).astype(BF16)


def _dft_dense_call(cm, sm, zc, zs, *, n, row0, count):
    tp = min(512, n)
    npt = n // tp
    blk0 = row0 // n
    return pl.pallas_call(
        _dft_kernel,
        out_shape=jax.ShapeDtypeStruct((count * n, F_W), BF16),
        grid=(npt, count),
        in_specs=[
            pl.BlockSpec((tp, n), lambda p, b: (p, 0)),
            pl.BlockSpec((tp, n), lambda p, b: (p, 0)),
            pl.BlockSpec((n, F_W), lambda p, b: (blk0 + b, 0)),
            pl.BlockSpec((n, F_W), lambda p, b: (blk0 + b, 0)),
        ],
        out_specs=pl.BlockSpec((tp, F_W), lambda p, b: (b * npt + p, 0)),
        compiler_params=_params(2),
        name="dft_dense",
    )(cm, sm, zc, zs)


def _flip_rows(flip_ref, src_ref, lo, hi, ft):
    return jnp.concatenate([_dot(flip_ref[...], src_ref[hi - (t + 1) * ft:hi - t * ft, :])
                            for t in range((hi - lo) // ft)], axis=0)


def _dft_folded_kernel(c_ref, s_ref, zc_ref, zs_ref, flip_ref, o_ref, ec_ref, os_ref, g_ref, mid_ref, *, tp, ft):
    n = zc_ref.shape[0]
    half = n // 2
    p = pl.program_id(1)
    scale = n ** -0.5

    @pl.when(p == 0)
    def _():
        row = lax.broadcasted_iota(jnp.int32, (half, F_W), 0)
        for z_ref, dst_ref, sign in ((zc_ref, ec_ref, 1.0), (zs_ref, os_ref, -1.0)):
            rev = _flip_rows(flip_ref, z_ref, half, n, ft)
            mirror = jnp.where(row == 0, 0.0, pltpu.roll(rev, 1, axis=0))
            folded = z_ref[0:half, :].astype(F32) + sign * mirror
            dst_ref[...] = folded.astype(BF16)
            if z_ref is zc_ref:
                alt_sum = jnp.sum(jnp.where((row & 1) == 0, folded, -folded), axis=0, keepdims=True)
                mid_ref[0:1, :] = (alt_sum + zc_ref[half:half + 1, :].astype(F32)) * scale

    rows = pl.ds(pl.multiple_of(p * tp, tp), tp)
    e = _dot(c_ref[rows, :], ec_ref[...])
    o = _dot(s_ref[rows, :], os_ref[...])
    k = p * tp + lax.broadcasted_iota(jnp.int32, (tp, F_W), 0)
    e = e + jnp.where((k & 1) == 0, scale, -scale) * zc_ref[half:half + 1, :].astype(F32)
    o_ref[rows, :] = (e - o).astype(BF16)
    g_ref[rows, :] = (e + o).astype(BF16)

    @pl.when(p == pl.num_programs(1) - 1)
    def _():
        row = lax.broadcasted_iota(jnp.int32, (half, F_W), 0)
        rev = _flip_rows(flip_ref, g_ref, 0, half, ft)
        upper = jnp.where(row == 0, mid_ref[0:1, :], pltpu.roll(rev, 1, axis=0))
        o_ref[half:n, :] = upper.astype(BF16)


def _dft_folded_call(cm, sm, zc, zs, *, n, count):
    half = n // 2
    tp = min(512, half)
    ft = min(512, half)
    flip = jnp.asarray(np.eye(ft)[::-1], BF16)
    seq_spec = pl.BlockSpec((n, F_W), lambda b, p: (b, 0))
    return pl.pallas_call(
        functools.partial(_dft_folded_kernel, tp=tp, ft=ft),
        out_shape=jax.ShapeDtypeStruct((count * n, F_W), BF16),
        grid=(count, half // tp),
        in_specs=[_const_spec((half, half)), _const_spec((half, half)), seq_spec, seq_spec, _const_spec((ft, ft))],
        out_specs=seq_spec,
        scratch_shapes=[pltpu.VMEM((half, F_W), BF16), pltpu.VMEM((half, F_W), BF16), pltpu.VMEM((half, F_W), BF16),
                        pltpu.VMEM((SUBLANES, F_W), F32)],
        compiler_params=_params(2),
        name="dft_folded",
    )(cm, sm, zc, zs, flip)


def _dft_mats(n, size=None):
    rows = cols = n if size is None else size
    k = jnp.arange(rows, dtype=jnp.int32)[:, None]
    nb = min(LANES, cols)
    na = cols // nb
    a = jnp.arange(na, dtype=jnp.int32)[None, :]
    b = jnp.arange(nb, dtype=jnp.int32)[None, :]
    w = 2.0 * np.pi / n
    ang1 = ((k * a * nb) % n).astype(F32) * w
    ang2 = ((k * b) % n).astype(F32) * w
    c1, s1, c2, s2 = jnp.cos(ang1), jnp.sin(ang1), jnp.cos(ang2), jnp.sin(ang2)
    scale = n ** -0.5
    cm = (c1[:, :, None] * c2[:, None, :] - s1[:, :, None] * s2[:, None, :]).reshape(rows, cols) * scale
    sm = (s1[:, :, None] * c2[:, None, :] + c1[:, :, None] * s2[:, None, :]).reshape(rows, cols) * scale
    return cm.astype(BF16), sm.astype(BF16)


def _channel_dft_mats():
    j = np.arange(F_GROUP_DIM)
    ang = 2.0 * np.pi * ((j[:, None] * j[None, :]) % F_GROUP_DIM) / F_GROUP_DIM
    eye = np.eye(F_GROUPS)
    scale = F_GROUP_DIM ** -0.5
    bdc = np.kron(eye, np.cos(ang) * scale)
    bds = np.kron(eye, np.sin(ang) * scale)
    return jnp.asarray(bdc, BF16), jnp.asarray(bds, BF16)


def _rope_tables(seq, pad_rows):
    rows = seq // GRID_W
    row = jnp.repeat(jnp.arange(rows), GRID_W).astype(F32)
    col = jnp.tile(jnp.arange(GRID_W), rows).astype(F32)
    inv = 1.0 / (ROPE_BASE ** (jnp.arange(0, AXIS_ROPE, 2, dtype=F32) / AXIS_ROPE))
    ar = row[:, None] * inv
    ac = col[:, None] * inv
    cos32 = jnp.concatenate([jnp.cos(ar), jnp.cos(ar), jnp.cos(ac), jnp.cos(ac)], axis=1)
    sin32 = jnp.concatenate([jnp.sin(ar), jnp.sin(ar), jnp.sin(ac), jnp.sin(ac)], axis=1)
    ones = jnp.ones((seq, QK_NOPE), F32)
    tail = HEAD_PAD - QK_NOPE - QK_ROPE
    cos_t = jnp.concatenate([ones, cos32, jnp.zeros((seq, tail), F32)], axis=1)
    sin_t = jnp.concatenate([jnp.zeros((seq, QK_NOPE), F32), sin32, jnp.zeros((seq, tail), F32)], axis=1)
    ident = jnp.concatenate([jnp.ones((pad_rows, QK_NOPE + QK_ROPE), F32), jnp.zeros((pad_rows, tail), F32)], axis=1)
    cos_t = jnp.concatenate([cos_t, ident], axis=0)
    sin_t = jnp.concatenate([sin_t, jnp.zeros((pad_rows, HEAD_PAD), F32)], axis=0)
    return cos_t, sin_t


def _rot_half(w):
    h = AXIS_ROPE // 2
    r1, r2, c1, c2 = w[..., 0:h], w[..., h:2 * h], w[..., 2 * h:3 * h], w[..., 3 * h:4 * h]
    return jnp.concatenate([-r2, r1, -c2, c1], axis=-1)


def _layer_weights(w_in, w_uq, w_ukv):
    d = w_in.shape[0]
    kr = w_in[:, OFF_KR:OFF_KR + QK_ROPE]
    w1 = jnp.concatenate([w_in[:, :OFF_KR], jnp.zeros((d, QK_NOPE), F32), kr, _rot_half(kr)], axis=1).astype(BF16)

    uq = w_uq.reshape(Q_LORA, MLA_HEADS, QK_NOPE + QK_ROPE)
    zq = jnp.zeros((Q_LORA, MLA_HEADS, HEAD_PAD - QK_NOPE - QK_ROPE), F32)
    wq_a = jnp.concatenate([uq, zq], axis=-1).reshape(Q_LORA, QK_W)
    wq_b = _rot_half(uq[..., QK_NOPE:]).reshape(Q_LORA, MLA_HEADS * QK_ROPE)
    wq = jnp.concatenate([wq_a, wq_b], axis=1).T.astype(BF16)

    ukv = w_ukv.reshape(KV_LORA, MLA_HEADS, QK_NOPE + V_DIM)
    wk = ukv[..., :QK_NOPE].reshape(KV_LORA, MLA_HEADS * QK_NOPE)
    wvt = ukv[..., QK_NOPE:].reshape(KV_LORA, MLA_OUT).T
    return w1, wq, wk.astype(BF16), wvt.astype(BF16)


def kernel(x, c, ctx, c_ctx, w_mod, b_mod, ffn1_norm, ffn1_w_in, ffn1_w_out, mix_norm, w_in, conv_w, q_norm, w_uq,
           kv_norm, w_ukv, w_out, ffn2_norm, ffn2_w_in, ffn2_w_out, final_norm):
    batch, seq, d = x.shape
    n_ctx = ctx.shape[1]
    depth = w_mod.shape[0]
    assert d == D_MODEL and seq % GRID_W == 0 and seq % n_ctx == 0 and n_ctx % SUBLANES == 0
    assert seq & (seq - 1) == 0 and n_ctx & (n_ctx - 1) == 0
    tm = 1024 if (seq % 1024 == 0 and (batch * n_ctx) % 1024 == 0) else n_ctx
    geo = dict(tm=tm, seq=seq, ctx=n_ctx, batch=batch)

    s = (x.reshape(batch * seq, d), ctx.reshape(batch * n_ctx, d))
    cond = jnp.concatenate([c, c_ctx[None, :]], axis=0)
    mods = _modulation(cond, w_mod, b_mod).reshape(depth, batch + 1, N_MOD, d)

    cos_t, sin_t = _rope_tables(seq, tm)
    cm_x, sm_x = _dft_mats(seq, seq // 2)
    cm_c, sm_c = _dft_mats(n_ctx)
    bdc, bds = _channel_dft_mats()
    ffn1_w = (ffn1_w_in.astype(BF16), ffn1_w_out.astype(BF16))
    ffn2_w = (ffn2_w_in.astype(BF16), ffn2_w_out.astype(BF16))
    w_o = w_out.astype(BF16)

    for l in range(depth):
        last = l == depth - 1
        m = mods[l]
        w1, wq, wk, wvt = _layer_weights(w_in[l], w_uq[l], w_ukv[l])
        s = _ffn_call(s, m, ffn1_norm[l], *ffn1_w, geo=geo, layer=l, mod_idx=0)
        zc, zs, gb, u, k, qt, vt = _proj_call(s, m, mix_norm[l], w1, q_norm[l], wq, kv_norm[l], wk, wvt,
                                              cos_t, sin_t, bdc, bds, geo=geo)
        att = _attn_call(qt, k, vt, geo=geo)
        fz = _dft_folded_call(cm_x, sm_x, zc, zs, n=seq, count=batch)
        if not last:
            att = (att, _attn_ctx_call(qt, k, vt, geo=geo))
            fz = (fz, _dft_dense_call(cm_c, sm_c, zc, zs, n=n_ctx, row0=batch * seq, count=batch))
        s = _ffn_call(s, m, ffn2_norm[l], *ffn2_w, geo=geo, layer=l, mod_idx=2,
                      mix=(fz, gb, u, att, conv_w[l], w_o),
                      final_g=final_norm if last else None, x_only=last)
    return s.reshape(batch, seq, d)
```
